```python
import jax, jax.numpy as jnp
from jax import lax
import numpy as np

D_MODEL = 1024
BATCH = 32
SEQ = 256
DEPTH = 2
DEC_BATCH = 8
DEC_SEQ = 1024
PAST_LEN = 256

GRID_W = 64
D_MIX = D_MODEL
ATTN_WIDTH = D_MIX // 2
POOL_WIDTH = D_MIX - ATTN_WIDTH
HEAD_DIM = 64
N_HEADS = ATTN_WIDTH // HEAD_DIM
N_KV_HEADS = 2
KV_WIDTH = N_KV_HEADS * HEAD_DIM
POOL_WINDOWS = (2, 4, 8, 16)
N_POOL_GROUPS = len(POOL_WINDOWS)
POOL_GC = POOL_WIDTH // N_POOL_GROUPS
IN_COLS = ATTN_WIDTH + 2 * KV_WIDTH + POOL_WIDTH
D_FF = 2816
N_EXPERTS = 8
TOP_K = 2
D_FF_EXPERT = 2816
N_DENSE = (DEPTH + 1) // 2
N_MOE = DEPTH // 2
Q_BLOCK = 128
ROPE_THETA = 10000.0
EPS = 1e-6

kernel_name = "hybrid_pool_gqa_prefix_diffusion_step"


def _rms_norm(x, w):
    xf = x.astype(jnp.float32)
    y = xf * lax.rsqrt(jnp.mean(xf * xf, axis=-1, keepdims=True) + EPS)
    return (y * w.astype(jnp.float32)).astype(x.dtype)


def _axial_rope(n):
    rows = n // GRID_W
    row = jnp.repeat(jnp.arange(rows, dtype=jnp.float32), GRID_W)
    col = jnp.tile(jnp.arange(GRID_W, dtype=jnp.float32), rows)
    nf = HEAD_DIM // 4
    inv = ROPE_THETA ** (-jnp.arange(nf, dtype=jnp.float32) / nf)
    ang = jnp.stack([row[:, None] * inv, col[:, None] * inv], axis=1)
    return jnp.cos(ang), jnp.sin(ang)


def _apply_rope(x, cos, sin):
    nf = HEAD_DIM // 4
    xr = x.reshape(x.shape[:-1] + (2, 2, nf))
    a, b = xr[..., 0, :], xr[..., 1, :]
    cs = cos.astype(x.dtype)[None, :, None]
    sn = sin.astype(x.dtype)[None, :, None]
    out = jnp.stack([a * cs - b * sn, b * cs + a * sn], axis=-2)
    return out.reshape(x.shape)


def _attend(q, k, v):
    B, Nq, H, D = q.shape
    KV = k.shape[2]
    G = H // KV
    scale = 1.0 / np.sqrt(D)
    qb = q.reshape(B, Nq // Q_BLOCK, Q_BLOCK, KV, G, D).transpose(1, 0, 2, 3, 4, 5)

    def block(qi):
        s = jnp.einsum('bqkgd,bskd->bkgqs', qi, k, preferred_element_type=jnp.float32) * scale
        p = jax.nn.softmax(s, axis=-1).astype(v.dtype)
        return jnp.einsum('bkgqs,bskd->bqkgd', p, v)

    o = lax.map(block, qb)
    return o.transpose(1, 0, 2, 3, 4, 5).reshape(B, Nq, H * D)


def _multiscale_pool(u, pool_w, pool_scale):
    B, N, _ = u.shape
    uf = u.reshape(B, N, N_POOL_GROUPS, POOL_GC).astype(jnp.float32)
    cs = jnp.concatenate([jnp.zeros((B, 1, N_POOL_GROUPS, POOL_GC), jnp.float32),
                          jnp.cumsum(uf, axis=1)], axis=1)
    t = jnp.arange(N)
    means = []
    for g, w in enumerate(POOL_WINDOWS):
        lo = jnp.clip(t - w // 2, 0, N)
        hi = jnp.clip(t + w // 2, 0, N)
        s = cs[:, hi, g] - cs[:, lo, g]
        means.append(s / (hi - lo).astype(jnp.float32)[None, :, None])
    d = (jnp.stack(means, axis=2) - uf).astype(u.dtype)
    y = jnp.einsum('bngc,gcd->bngd', d, pool_w)
    return y.reshape(B, N, POOL_WIDTH) * pool_scale


def _mixer(h, w_in, w_o, q_norm_w, k_norm_w, pool_w, pool_scale, rope, k_ctx, v_ctx):
    B, N, _ = h.shape
    proj = h @ w_in
    q = proj[..., :ATTN_WIDTH].reshape(B, N, N_HEADS, HEAD_DIM)
    k = proj[..., ATTN_WIDTH:ATTN_WIDTH + KV_WIDTH].reshape(B, N, N_KV_HEADS, HEAD_DIM)
    v = proj[..., ATTN_WIDTH + KV_WIDTH:ATTN_WIDTH + 2 * KV_WIDTH].reshape(B, N, N_KV_HEADS, HEAD_DIM)
    u = proj[..., ATTN_WIDTH + 2 * KV_WIDTH:]
    q = _rms_norm(q, q_norm_w)
    k = _rms_norm(k, k_norm_w)
    if rope is None:
        attn = _attend(q, k, v)
    else:
        q = _apply_rope(q, *rope)
        keys = jnp.concatenate([_apply_rope(k, *rope), k_ctx], axis=1)
        vals = jnp.concatenate([v, v_ctx], axis=1)
        attn = _attend(q, keys, vals)
    pool = _multiscale_pool(u, pool_w, pool_scale)
    out = jnp.concatenate([attn, pool], axis=-1) @ w_o
    return out, k, v


def _swiglu(h, w1, w3, w2):
    return (jax.nn.silu(h @ w1) * (h @ w3)) @ w2


def _moe(h, router_w, router_b, w1, w3, w2):
    B, N, D = h.shape
    t = h.reshape(B * N, D)
    logits = (t @ router_w).astype(jnp.float32) + router_b.astype(jnp.float32)
    top_v, top_i = lax.top_k(logits, TOP_K)
    wts = jax.nn.softmax(top_v, axis=-1)
    gate = jnp.sum(jax.nn.one_hot(top_i, N_EXPERTS, dtype=jnp.float32) * wts[..., None], axis=1)
    gate = gate.astype(t.dtype)
    out = jnp.zeros_like(t)
    for e in range(N_EXPERTS):
        out = out + gate[:, e:e + 1] * _swiglu(t, w1[e], w3[e], w2[e])
    return out.reshape(B, N, D)


def _trunk(x, cond, rope, ctx_k, ctx_v, norm1_w, norm2_w, ada_w, ada_b, w_in, w_o,
           q_norm_w, k_norm_w, pool_w, pool_scale, ffn_w1, ffn_w3, ffn_w2,
           router_w, router_b, moe_w1, moe_w3, moe_w2):
    ks, vs = [], []
    s_cond = jax.nn.silu(cond)
    for l in range(DEPTH):
        mod = (s_cond @ ada_w[l] + ada_b[l])[:, None, :]
        sh1, sc1, g1, sh2, sc2, g2 = jnp.split(mod, 6, axis=-1)
        h = _rms_norm(x, norm1_w[l]) * (1 + sc1) + sh1
        kc = None if ctx_k is None else ctx_k[:, l]
        vc = None if ctx_v is None else ctx_v[:, l]
        mix_out, k, v = _mixer(h, w_in[l], w_o[l], q_norm_w[l], k_norm_w[l], pool_w[l],
                               pool_scale[l], rope, kc, vc)
        x = x + g1 * mix_out
        h = _rms_norm(x, norm2_w[l]) * (1 + sc2) + sh2
        if l % 2 == 0:
            f = _swiglu(h, ffn_w1[l // 2], ffn_w3[l // 2], ffn_w2[l // 2])
        else:
            f = _moe(h, router_w[l // 2], router_b[l // 2], moe_w1[l // 2], moe_w3[l // 2], moe_w2[l // 2])
        x = x + g2 * f
        ks.append(k)
        vs.append(v)
    return x, ks, vs


def setup_inputs(seed: int = 0) -> dict:
    key = jax.random.key(seed)
    ks = jax.random.split(key, 32)
    f32 = jnp.float32

    def nrm(k, shape, scale=1.0):
        return jax.random.normal(k, shape, f32) * scale

    D = D_MODEL
    return {
        "x_prompt": nrm(ks[0], (BATCH, SEQ, D)),
        "x_sample": nrm(ks[1], (DEC_BATCH, DEC_SEQ, D)),
        "cache_k": nrm(ks[2], (DEC_BATCH, DEPTH, PAST_LEN, N_KV_HEADS, HEAD_DIM)),
        "cache_v": nrm(ks[3], (DEC_BATCH, DEPTH, PAST_LEN, N_KV_HEADS, HEAD_DIM)),
        "c": nrm(ks[4], (DEC_BATCH, D)),
        "c_ctx": nrm(ks[5], (D,)),
        "norm1_w": 1.0 + nrm(ks[6], (DEPTH, D), 0.05),
        "norm2_w": 1.0 + nrm(ks[7], (DEPTH, D), 0.05),
        "ada_w": nrm(ks[8], (DEPTH, D, 6 * D), 0.5 * D ** -0.5),
        "ada_b": nrm(ks[9], (DEPTH, 6 * D), 0.02),
        "w_in": nrm(ks[10], (DEPTH, D, IN_COLS), D ** -0.5),
        "w_o": nrm(ks[11], (DEPTH, D_MIX, D), D_MIX ** -0.5),
        "q_norm_w": 1.0 + nrm(ks[12], (DEPTH, HEAD_DIM), 0.05),
        "k_norm_w": 1.0 + nrm(ks[13], (DEPTH, HEAD_DIM), 0.05),
        "pool_w": nrm(ks[14], (DEPTH, N_POOL_GROUPS, POOL_GC, POOL_GC), POOL_GC ** -0.5),
        "pool_scale": 1.0 + nrm(ks[15], (DEPTH, POOL_WIDTH), 0.1),
        "ffn_w1": nrm(ks[16], (N_DENSE, D, D_FF), D ** -0.5),
        "ffn_w3": nrm(ks[17], (N_DENSE, D, D_FF), D ** -0.5),
        "ffn_w2": nrm(ks[18], (N_DENSE, D_FF, D), D_FF ** -0.5),
        "router_w": nrm(ks[19], (N_MOE, D, N_EXPERTS), D ** -0.5),
        "router_b": nrm(ks[20], (N_MOE, N_EXPERTS), 0.01),
        "moe_w1": nrm(ks[21], (N_MOE, N_EXPERTS, D, D_FF_EXPERT), D ** -0.5),
        "moe_w3": nrm(ks[22], (N_MOE, N_EXPERTS, D, D_FF_EXPERT), D ** -0.5),
        "moe_w2": nrm(ks[23], (N_MOE, N_EXPERTS, D_FF_EXPERT, D), D_FF_EXPERT ** -0.5),
    }


def reference(x_prompt, x_sample, cache_k, cache_v, c, c_ctx, norm1_w, norm2_w, ada_w, ada_b,
              w_in, w_o, q_norm_w, k_norm_w, pool_w, pool_scale, ffn_w1, ffn_w3, ffn_w2,
              router_w, router_b, moe_w1, moe_w3, moe_w2):
    y_prompt, kp, vp = _trunk(x_prompt, c_ctx[None, :], None, None, None,
                              norm1_w, norm2_w, ada_w, ada_b, w_in, w_o, q_norm_w, k_norm_w,
                              pool_w, pool_scale, ffn_w1, ffn_w3, ffn_w2,
                              router_w, router_b, moe_w1, moe_w3, moe_w2)
    new_k = jnp.stack(kp, axis=1)
    new_v = jnp.stack(vp, axis=1)
    rope = _axial_rope(x_sample.shape[1])
    y_sample, _, _ = _trunk(x_sample, c, rope, cache_k, cache_v,
                            norm1_w, norm2_w, ada_w, ada_b, w_in, w_o, q_norm_w, k_norm_w,
                            pool_w, pool_scale, ffn_w1, ffn_w3, ffn_w2,
                            router_w, router_b, moe_w1, moe_w3, moe_w2)
    return (y_prompt, y_sample, new_k, new_v)
```

```python
import functools

import jax
import jax.numpy as jnp
from jax import lax
from jax.experimental import pallas as pl
from jax.experimental.pallas import tpu as pltpu

F32 = jnp.float32
BF16 = jnp.bfloat16

D_MODEL = 1024
HEAD_DIM = 64
N_HEADS = 8
N_KV_HEADS = 2
ATTN_WIDTH = N_HEADS * HEAD_DIM
KV_WIDTH = N_KV_HEADS * HEAD_DIM
POOL_WIDTH = D_MODEL - ATTN_WIDTH
N_POOL_GROUPS = 4
POOL_GC = POOL_WIDTH // N_POOL_GROUPS
IN_COLS = ATTN_WIDTH + 2 * KV_WIDTH + POOL_WIDTH
D_FF = 2816
N_EXPERTS = 8
GRID_W = 64
ROPE_THETA = 10000.0
EPS = 1e-6
N_MOD = 6

LANES = 128
SEQ_TILE = 256
FFN_TILE = 512
FF_CHUNK = 1408
ROUTE_TILE = 512
MOVE_TILE = 256
COND_ROWS = 16
VMEM_LIMIT = 48 * 1024 * 1024
NEG_BIG = -1e30


def _cond_of_tile(i, tile, t_ctx, dec_seq):
    tok = i * tile
    return jnp.where(tok < t_ctx, 0, 1 + (tok - t_ctx) // dec_seq)


def _dot(a, b):
    return jnp.dot(a, b, preferred_element_type=F32)


def _dot_nt(a, b):
    return lax.dot_general(a, b, (((1,), (1,)), ((), ())), preferred_element_type=F32)


def _split_bf16(x):
    hi = x.astype(BF16)
    lo = (x - hi.astype(F32)).astype(BF16)
    return hi, lo


def _rms_mod(x, nw, scale, shift):
    ms = jnp.mean(x * x, axis=-1, keepdims=True)
    y = x * lax.rsqrt(ms + EPS) * nw
    return y * (1.0 + scale) + shift


def _ada_kernel(c_ref, w_ref, b_ref, o_ref):
    c = c_ref[...]
    s = c * jax.nn.sigmoid(c)
    hi, lo = _split_bf16(s)
    w = w_ref[...]
    whi, wlo = _split_bf16(w)
    o_ref[...] = _dot(hi, whi) + _dot(lo, whi) + _dot(hi, wlo) + b_ref[...]


def _ada(cond, ada_w, ada_b):
    depth = ada_w.shape[0]
    cols = ada_w.shape[2]
    tn = cols // 4
    return pl.pallas_call(
        _ada_kernel,
        out_shape=jax.ShapeDtypeStruct((depth, COND_ROWS, cols), F32),
        grid=(depth, cols // tn),
        in_specs=[
            pl.BlockSpec((COND_ROWS, D_MODEL), lambda l, n: (0, 0)),
            pl.BlockSpec((None, D_MODEL, tn), lambda l, n: (l, 0, n)),
            pl.BlockSpec((None, 1, tn), lambda l, n: (l, 0, n)),
        ],
        out_specs=pl.BlockSpec((None, COND_ROWS, tn), lambda l, n: (l, 0, n)),
        compiler_params=pltpu.CompilerParams(vmem_limit_bytes=VMEM_LIMIT),
        name="ada_mod",
    )(cond, ada_w, ada_b.reshape(depth, 1, cols))


def _head_rms(t, bd, w):
    hi, lo = _split_bf16(t * t)
    ss = _dot(hi, bd) + _dot(lo, bd)
    return t * lax.rsqrt(ss * (1.0 / HEAD_DIM) + EPS) * w


def _rope(t, cos, sin):
    width = t.shape[1]
    lane = lax.broadcasted_iota(jnp.int32, t.shape, 1)
    first = (lane % 32) < 16
    partner = jnp.where(first, pltpu.roll(t, width - 16, 1), pltpu.roll(t, 16, 1))
    return t * cos + partner * sin


def _in_kernel(x_ref, mod_ref, nw_ref, w_ref, qw_ref, kw_ref, bd_ref, cos_ref, sin_ref,
               q_ref, k_ref, v_ref, u_ref):
    h = _rms_mod(x_ref[...], nw_ref[...], mod_ref[1:2, :], mod_ref[0:1, :])
    proj = _dot(h.astype(BF16), w_ref[...])
    q = proj[:, :ATTN_WIDTH]
    k = proj[:, ATTN_WIDTH:ATTN_WIDTH + KV_WIDTH]
    v_ref[...] = proj[:, ATTN_WIDTH + KV_WIDTH:ATTN_WIDTH + 2 * KV_WIDTH]
    u_ref[...] = proj[:, ATTN_WIDTH + 2 * KV_WIDTH:]
    cos = cos_ref[...]
    sin = sin_ref[...]
    q = _head_rms(q, bd_ref[...], qw_ref[...])
    k = _head_rms(k, bd_ref[:KV_WIDTH, :KV_WIDTH], kw_ref[...])
    q_ref[...] = _rope(q, cos, sin) * (HEAD_DIM ** -0.5)
    k_ref[...] = _rope(k, cos[:, :KV_WIDTH], sin[:, :KV_WIDTH])


def _in_proj(l, x, mod, norm_w, w_in, qw, kw, bd, cos_t, sin_t, t_ctx, dec_seq):
    t = x.shape[0]
    n_rope = dec_seq // SEQ_TILE

    def rope_idx(i):
        return (jnp.where(i * SEQ_TILE < t_ctx, 0, 1 + (i - t_ctx // SEQ_TILE) % n_rope), 0)

    row = lambda i: (i, 0)
    return pl.pallas_call(
        _in_kernel,
        out_shape=(jax.ShapeDtypeStruct((t, ATTN_WIDTH), F32),
                   jax.ShapeDtypeStruct((t, KV_WIDTH), F32),
                   jax.ShapeDtypeStruct((t, KV_WIDTH), F32),
                   jax.ShapeDtypeStruct((t, POOL_WIDTH), F32)),
        grid=(t // SEQ_TILE,),
        in_specs=[
            pl.BlockSpec((SEQ_TILE, D_MODEL), row),
            pl.BlockSpec((None, None, N_MOD, D_MODEL),
                         lambda i: (l, _cond_of_tile(i, SEQ_TILE, t_ctx, dec_seq), 0, 0)),
            pl.BlockSpec((None, 1, D_MODEL), lambda i: (l, 0, 0)),
            pl.BlockSpec((None, D_MODEL, IN_COLS), lambda i: (l, 0, 0)),
            pl.BlockSpec((None, 1, ATTN_WIDTH), lambda i: (l, 0, 0)),
            pl.BlockSpec((None, 1, KV_WIDTH), lambda i: (l, 0, 0)),
            pl.BlockSpec((ATTN_WIDTH, ATTN_WIDTH), lambda i: (0, 0)),
            pl.BlockSpec((SEQ_TILE, ATTN_WIDTH), rope_idx),
            pl.BlockSpec((SEQ_TILE, ATTN_WIDTH), rope_idx),
        ],
        out_specs=(pl.BlockSpec((SEQ_TILE, ATTN_WIDTH), row),
                   pl.BlockSpec((SEQ_TILE, KV_WIDTH), row),
                   pl.BlockSpec((SEQ_TILE, KV_WIDTH), row),
                   pl.BlockSpec((SEQ_TILE, POOL_WIDTH), row)),
        compiler_params=pltpu.CompilerParams(vmem_limit_bytes=VMEM_LIMIT),
        name=f"in_proj_{l}",
    )(x, mod, norm_w, w_in, qw, kw, bd, cos_t, sin_t)


def _kv_halves(t, g):
    lane = lax.broadcasted_iota(jnp.int32, t.shape, 1)
    base = jnp.where((lane // HEAD_DIM) == g, t, 0.0)
    both = base + pltpu.roll(base, HEAD_DIM, 1)
    low = jnp.where(lane < HEAD_DIM, both, 0.0).astype(BF16)
    high = jnp.where(lane >= HEAD_DIM, both, 0.0).astype(BF16)
    return low, high


def _pair_attention(q, keys, vals, g):
    qb = q.astype(BF16)
    ks = [_kv_halves(k, g) for k in keys]
    vs = [_kv_halves(v, g) for v in vals]
    out = None
    for half in range(2):
        scores = [_dot_nt(qb, k[half]) for k in ks]
        m = functools.reduce(jnp.maximum, [jnp.max(s, axis=-1, keepdims=True) for s in scores])
        ps = [jnp.exp(s - m) for s in scores]
        denom = functools.reduce(jnp.add, [jnp.sum(p, axis=-1, keepdims=True) for p in ps])
        o = functools.reduce(jnp.add, [_dot(p.astype(BF16), v[half]) for p, v in zip(ps, vs)])
        o = o / denom
        out = o if out is None else out + o
    return out


def _attn_ctx_kernel(q_ref, k_ref, v_ref, o_ref):
    g = pl.program_id(1) // (N_HEADS // N_KV_HEADS // 2)
    o_ref[...] = _pair_attention(q_ref[...], [k_ref[...]], [v_ref[...]], g)


def _attn_lat_kernel(q_ref, k_ref, v_ref, ck_ref, cv_ref, o_ref):
    g = pl.program_id(1) // (N_HEADS // N_KV_HEADS // 2)
    o_ref[...] = _pair_attention(q_ref[...], [k_ref[...], ck_ref[...]], [v_ref[...], cv_ref[...]], g)


def _attention(l, q, k, v, cache_k, cache_v, n_ctx, seq, n_lat, dec_seq):
    t_ctx = n_ctx * seq
    n_pairs = ATTN_WIDTH // LANES
    ctx_out = pl.pallas_call(
        _attn_ctx_kernel,
        out_shape=jax.ShapeDtypeStruct((t_ctx, ATTN_WIDTH), F32),
        grid=(n_ctx, n_pairs),
        in_specs=[
            pl.BlockSpec((seq, LANES), lambda b, j: (b, j)),
            pl.BlockSpec((seq, KV_WIDTH), lambda b, j: (b, 0)),
            pl.BlockSpec((seq, KV_WIDTH), lambda b, j: (b, 0)),
        ],
        out_specs=pl.BlockSpec((seq, LANES), lambda b, j: (b, j)),
        compiler_params=pltpu.CompilerParams(vmem_limit_bytes=VMEM_LIMIT),
        name=f"attn_ctx_{l}",
    )(q, k, v)

    past = cache_k.shape[2]
    q_tiles = dec_seq // SEQ_TILE
    lat_row0 = t_ctx // SEQ_TILE
    lat_seq0 = t_ctx // dec_seq
    lat_out = pl.pallas_call(
        _attn_lat_kernel,
        out_shape=jax.ShapeDtypeStruct((n_lat * dec_seq, ATTN_WIDTH), F32),
        grid=(n_lat, n_pairs, q_tiles),
        in_specs=[
            pl.BlockSpec((SEQ_TILE, LANES), lambda b, j, r: (lat_row0 + b * q_tiles + r, j)),
            pl.BlockSpec((dec_seq, KV_WIDTH), lambda b, j, r: (lat_seq0 + b, 0)),
            pl.BlockSpec((dec_seq, KV_WIDTH), lambda b, j, r: (lat_seq0 + b, 0)),
            pl.BlockSpec((None, None, past, KV_WIDTH), lambda b, j, r: (b, l, 0, 0)),
            pl.BlockSpec((None, None, past, KV_WIDTH), lambda b, j, r: (b, l, 0, 0)),
        ],
        out_specs=pl.BlockSpec((SEQ_TILE, LANES), lambda b, j, r: (b * q_tiles + r, j)),
        compiler_params=pltpu.CompilerParams(vmem_limit_bytes=VMEM_LIMIT),
        name=f"attn_lat_{l}",
    )(q, k, v, cache_k, cache_v)
    return ctx_out, lat_out


def _pool_kernel(u_ref, pw_ref, ps_ref, o_ref):
    u = u_ref[...]
    n = u.shape[0]
    half = jnp.left_shift(1, pl.program_id(1))
    row = lax.broadcasted_iota(jnp.int32, (n, n), 0)
    col = lax.broadcasted_iota(jnp.int32, (n, n), 1)
    band = jnp.where((col >= row - half) & (col < row + half), 1.0, 0.0).astype(BF16)
    hi, lo = _split_bf16(u)
    window_sum = _dot(band, hi) + _dot(band, lo)
    pos = lax.broadcasted_iota(jnp.int32, u.shape, 0)
    count = jnp.minimum(pos + half, n) - jnp.maximum(pos - half, 0)
    d = window_sum / count.astype(F32) - u
    o_ref[...] = _dot(d.astype(BF16), pw_ref[...]) * ps_ref[...]


def _pool(l, u, pool_w, pool_scale, n_ctx, seq, n_lat, dec_seq):
    lat_seq0 = (n_ctx * seq) // dec_seq
    w_spec = pl.BlockSpec((None, None, POOL_GC, POOL_GC), lambda b, g: (l, g, 0, 0))
    s_spec = pl.BlockSpec((None, 1, POOL_GC), lambda b, g: (l, 0, g))
    ctx_out = pl.pallas_call(
        _pool_kernel,
        out_shape=jax.ShapeDtypeStruct((n_ctx * seq, POOL_WIDTH), F32),
        grid=(n_ctx, N_POOL_GROUPS),
        in_specs=[pl.BlockSpec((seq, POOL_GC), lambda b, g: (b, g)), w_spec, s_spec],
        out_specs=pl.BlockSpec((seq, POOL_GC), lambda b, g: (b, g)),
        compiler_params=pltpu.CompilerParams(vmem_limit_bytes=VMEM_LIMIT),
        name=f"pool_ctx_{l}",
    )(u, pool_w, pool_scale)
    lat_out = pl.pallas_call(
        _pool_kernel,
        out_shape=jax.ShapeDtypeStruct((n_lat * dec_seq, POOL_WIDTH), F32),
        grid=(n_lat, N_POOL_GROUPS),
        in_specs=[pl.BlockSpec((dec_seq, POOL_GC), lambda b, g: (lat_seq0 + b, g)), w_spec, s_spec],
        out_specs=pl.BlockSpec((dec_seq, POOL_GC), lambda b, g: (b, g)),
        compiler_params=pltpu.CompilerParams(vmem_limit_bytes=VMEM_LIMIT),
        name=f"pool_lat_{l}",
    )(u, pool_w, pool_scale)
    return ctx_out, lat_out


def _out_kernel(ac_ref, al_ref, pc_ref, pl_ref, x_ref, mod_ref, nw_ref, w_ref, x1_ref, h2_ref, *, ctx_tiles):
    is_ctx = pl.program_id(0) < ctx_tiles
    a = jnp.where(is_ctx, ac_ref[...], al_ref[...])
    p = jnp.where(is_ctx, pc_ref[...], pl_ref[...])
    mix = _dot(a.astype(BF16), w_ref[:ATTN_WIDTH, :]) + _dot(p.astype(BF16), w_ref[ATTN_WIDTH:, :])
    x1 = x_ref[...] + mod_ref[2:3, :] * mix
    x1_ref[...] = x1
    h2_ref[...] = _rms_mod(x1, nw_ref[...], mod_ref[4:5, :], mod_ref[3:4, :]).astype(h2_ref.dtype)


def _out_proj(l, attn, pool, x, mod, norm_w, w_o, h2_dtype, t_ctx, dec_seq):
    t = x.shape[0]
    ctx_tiles = t_ctx // SEQ_TILE
    row = lambda i: (i, 0)
    ctx_row = lambda i: (jnp.minimum(i, ctx_tiles - 1), 0)
    lat_row = lambda i: (jnp.maximum(i - ctx_tiles, 0), 0)
    return pl.pallas_call(
        functools.partial(_out_kernel, ctx_tiles=ctx_tiles),
        out_shape=(jax.ShapeDtypeStruct((t, D_MODEL), F32),
                   jax.ShapeDtypeStruct((t, D_MODEL), h2_dtype)),
        grid=(t // SEQ_TILE,),
        in_specs=[
            pl.BlockSpec((SEQ_TILE, ATTN_WIDTH), ctx_row),
            pl.BlockSpec((SEQ_TILE, ATTN_WIDTH), lat_row),
            pl.BlockSpec((SEQ_TILE, POOL_WIDTH), ctx_row),
            pl.BlockSpec((SEQ_TILE, POOL_WIDTH), lat_row),
            pl.BlockSpec((SEQ_TILE, D_MODEL), row),
            pl.BlockSpec((None, None, N_MOD, D_MODEL),
                         lambda i: (l, _cond_of_tile(i, SEQ_TILE, t_ctx, dec_seq), 0, 0)),
            pl.BlockSpec((None, 1, D_MODEL), lambda i: (l, 0, 0)),
            pl.BlockSpec((None, D_MODEL, D_MODEL), lambda i: (l, 0, 0)),
        ],
        out_specs=(pl.BlockSpec((SEQ_TILE, D_MODEL), row),
                   pl.BlockSpec((SEQ_TILE, D_MODEL), row)),
        compiler_params=pltpu.CompilerParams(vmem_limit_bytes=VMEM_LIMIT),
        name=f"out_proj_{l}",
    )(attn[0], attn[1], pool[0], pool[1], x, mod, norm_w, w_o)


def _swiglu_step(h, w1_ref, w3_ref, w2_ref, acc_ref):
    a = _dot(h, w1_ref[...])
    b = _dot(h, w3_ref[...])
    gated = (a * jax.nn.sigmoid(a) * b).astype(BF16)
    acc_ref[...] += _dot(gated, w2_ref[...])


def _ffn_kernel(h_ref, w1_ref, w3_ref, w2_ref, x1_ref, mod_ref, o_ref, acc_ref):
    f = pl.program_id(1)

    @pl.when(f == 0)
    def _():
        acc_ref[...] = jnp.zeros_like(acc_ref)

    _swiglu_step(h_ref[...], w1_ref, w3_ref, w2_ref, acc_ref)

    @pl.when(f == pl.num_programs(1) - 1)
    def _():
        o_ref[...] = x1_ref[...] + mod_ref[5:6, :] * acc_ref[...]


def _dense_ffn(l, h2, x1, mod, w1, w3, w2, t_ctx, dec_seq):
    t = x1.shape[0]
    row = lambda i, f: (i, 0)
    return pl.pallas_call(
        _ffn_kernel,
        out_shape=jax.ShapeDtypeStruct((t, D_MODEL), F32),
        grid=(t // FFN_TILE, D_FF // FF_CHUNK),
        in_specs=[
            pl.BlockSpec((FFN_TILE, D_MODEL), row),
            pl.BlockSpec((None, D_MODEL, FF_CHUNK), lambda i, f: (l // 2, 0, f)),
            pl.BlockSpec((None, D_MODEL, FF_CHUNK), lambda i, f: (l // 2, 0, f)),
            pl.BlockSpec((None, FF_CHUNK, D_MODEL), lambda i, f: (l // 2, f, 0)),
            pl.BlockSpec((FFN_TILE, D_MODEL), row),
            pl.BlockSpec((None, None, N_MOD, D_MODEL),
                         lambda i, f: (l, _cond_of_tile(i, FFN_TILE, t_ctx, dec_seq), 0, 0)),
        ],
        out_specs=pl.BlockSpec((FFN_TILE, D_MODEL), row),
        scratch_shapes=[pltpu.VMEM((FFN_TILE, D_MODEL), F32)],
        compiler_params=pltpu.CompilerParams(vmem_limit_bytes=VMEM_LIMIT),
        name=f"dense_ffn_{l}",
    )(h2, w1, w3, w2, x1, mod)


def _route_kernel(h_ref, rw_ref, rb_ref, ids_ref, wts_ref, cnt_ref, carry_ref):
    @pl.when(pl.program_id(0) == 0)
    def _():
        carry_ref[...] = jnp.zeros_like(carry_ref)

    h = h_ref[...]
    hi, lo = _split_bf16(h)
    rw = rw_ref[...]
    whi, wlo = _split_bf16(rw)
    logits = _dot(hi, whi) + _dot(lo, whi) + _dot(hi, wlo) + rb_ref[...]
    lane = lax.broadcasted_iota(jnp.int32, logits.shape, 1)
    lane_f = lane.astype(F32)
    m0 = jnp.max(logits, axis=-1, keepdims=True)
    e0 = jnp.min(jnp.where(logits == m0, lane_f, float(LANES)), axis=-1, keepdims=True)
    rest = jnp.where(lane_f == e0, -jnp.inf, logits)
    m1 = jnp.max(rest, axis=-1, keepdims=True)
    e1 = jnp.min(jnp.where(rest == m1, lane_f, float(LANES)), axis=-1, keepdims=True)
    ex = jnp.exp(m1 - m0)
    w0 = 1.0 / (1.0 + ex)
    w1 = ex / (1.0 + ex)

    n = logits.shape[0]
    sel0 = lane_f == e0
    sel1 = lane_f == e1
    mask = jnp.where(sel0 | sel1, 1.0, 0.0)
    r = lax.broadcasted_iota(jnp.int32, (n, n), 0)
    c = lax.broadcasted_iota(jnp.int32, (n, n), 1)
    below = jnp.where(c < r, 1.0, 0.0).astype(BF16)
    rank = _dot(below, mask.astype(BF16)) + carry_ref[0:1, :]
    r0 = jnp.sum(jnp.where(sel0, rank, 0.0), axis=-1, keepdims=True).astype(jnp.int32)
    r1 = jnp.sum(jnp.where(sel1, rank, 0.0), axis=-1, keepdims=True).astype(jnp.int32)
    carry_ref[...] = carry_ref[...] + jnp.sum(mask, axis=0, keepdims=True)
    cnt_ref[...] = carry_ref[...]

    ids_ref[...] = jnp.where(lane == 0, e0.astype(jnp.int32), jnp.where(lane == 1, e1.astype(jnp.int32),
                             jnp.where(lane == 2, r0, jnp.where(lane == 3, r1, 0))))
    wts_ref[...] = jnp.where(lane == 0, w0, jnp.where(lane == 1, w1, 0.0))


def _route(h2, router_w, router_b):
    t = h2.shape[0]
    rw = jnp.zeros((D_MODEL, LANES), F32).at[:, :N_EXPERTS].set(router_w)
    rb = jnp.full((1, LANES), NEG_BIG, F32).at[0, :N_EXPERTS].set(router_b)
    row = lambda i: (i, 0)
    return pl.pallas_call(
        _route_kernel,
        out_shape=(jax.ShapeDtypeStruct((t, LANES), jnp.int32),
                   jax.ShapeDtypeStruct((t, LANES), F32),
                   jax.ShapeDtypeStruct((8, LANES), F32)),
        grid=(t // ROUTE_TILE,),
        in_specs=[
            pl.BlockSpec((ROUTE_TILE, D_MODEL), row),
            pl.BlockSpec((D_MODEL, LANES), lambda i: (0, 0)),
            pl.BlockSpec((1, LANES), lambda i: (0, 0)),
        ],
        out_specs=(pl.BlockSpec((ROUTE_TILE, LANES), row),
                   pl.BlockSpec((ROUTE_TILE, LANES), row),
                   pl.BlockSpec((8, LANES), lambda i: (0, 0))),
        scratch_shapes=[pltpu.VMEM((8, LANES), F32)],
        compiler_params=pltpu.CompilerParams(dimension_semantics=("arbitrary",),
                                             vmem_limit_bytes=VMEM_LIMIT),
        name="route",
    )(h2, rw, rb)


def _row_copy(src_ref, src_row, dst_ref, dst_row, sem):
    return pltpu.make_async_copy(src_ref.at[pl.ds(src_row, 1)], dst_ref.at[pl.ds(dst_row, 1)], sem)


def _dispatch_kernel(p0_ref, p1_ref, h_ref, init_ref, xs_ref, sem):
    del init_ref
    base = pl.program_id(0) * MOVE_TILE

    def copies(r):
        return (_row_copy(h_ref, r, xs_ref, p0_ref[base + r], sem),
                _row_copy(h_ref, r, xs_ref, p1_ref[base + r], sem))

    def start(r, carry):
        for cp in copies(r):
            cp.start()
        return carry

    def wait(r, carry):
        for cp in copies(r):
            cp.wait()
        return carry

    lax.fori_loop(0, MOVE_TILE, start, 0)
    lax.fori_loop(0, MOVE_TILE, wait, 0)


def _dispatch(h2, pos0, pos1, n_slots):
    t = h2.shape[0]
    xs_init = jnp.zeros((n_slots, D_MODEL), F32)
    return pl.pallas_call(
        _dispatch_kernel,
        out_shape=jax.ShapeDtypeStruct((n_slots, D_MODEL), F32),
        grid_spec=pltpu.PrefetchScalarGridSpec(
            num_scalar_prefetch=2,
            grid=(t // MOVE_TILE,),
            in_specs=[pl.BlockSpec((MOVE_TILE, D_MODEL), lambda i, p0, p1: (i, 0)),
                      pl.BlockSpec(memory_space=pl.ANY)],
            out_specs=pl.BlockSpec(memory_space=pl.ANY),
            scratch_shapes=[pltpu.SemaphoreType.DMA],
        ),
        input_output_aliases={3: 0},
        compiler_params=pltpu.CompilerParams(dimension_semantics=("arbitrary",),
                                             vmem_limit_bytes=VMEM_LIMIT),
        name="dispatch",
    )(pos0, pos1, h2, xs_init)


def _expert_kernel(te_ref, tv_ref, x_ref, w1_ref, w3_ref, w2_ref, o_ref, acc_ref):
    i = pl.program_id(0)
    f = pl.program_id(1)
    last = pl.num_programs(1) - 1

    @pl.when(f == 0)
    def _():
        acc_ref[...] = jnp.zeros_like(acc_ref)

    @pl.when(tv_ref[i] == 1)
    def _():
        _swiglu_step(x_ref[...].astype(BF16), w1_ref, w3_ref, w2_ref, acc_ref)

    @pl.when(f == last)
    def _():
        o_ref[...] = acc_ref[...]


def _expert_ffn(xs, tile_expert, tile_valid, w1, w3, w2):
    n_slots = xs.shape[0]
    row = lambda i, f, te, tv: (i, 0)
    return pl.pallas_call(
        _expert_kernel,
        out_shape=jax.ShapeDtypeStruct((n_slots, D_MODEL), F32),
        grid_spec=pltpu.PrefetchScalarGridSpec(
            num_scalar_prefetch=2,
            grid=(n_slots // FFN_TILE, D_FF // FF_CHUNK),
            in_specs=[
                pl.BlockSpec((FFN_TILE, D_MODEL), row),
                pl.BlockSpec((None, D_MODEL, FF_CHUNK), lambda i, f, te, tv: (te[i], 0, f)),
                pl.BlockSpec((None, D_MODEL, FF_CHUNK), lambda i, f, te, tv: (te[i], 0, f)),
                pl.BlockSpec((None, FF_CHUNK, D_MODEL), lambda i, f, te, tv: (te[i], f, 0)),
            ],
            out_specs=pl.BlockSpec((FFN_TILE, D_MODEL), row),
            scratch_shapes=[pltpu.VMEM((FFN_TILE, D_MODEL), F32)],
        ),
        compiler_params=pltpu.CompilerParams(vmem_limit_bytes=VMEM_LIMIT),
        name="expert_ffn",
    )(tile_expert, tile_valid, xs, w1, w3, w2)


def _combine_kernel(p0_ref, p1_ref, ys_ref, x1_ref, mod_ref, wts_ref, o_ref, y0_ref, y1_ref, sem):
    base = pl.program_id(0) * MOVE_TILE

    def copies(r):
        return (_row_copy(ys_ref, p0_ref[base + r], y0_ref, r, sem),
                _row_copy(ys_ref, p1_ref[base + r], y1_ref, r, sem))

    def start(r, carry):
        for cp in copies(r):
            cp.start()
        return carry

    def wait(r, carry):
        for cp in copies(r):
            cp.wait()
        return carry

    lax.fori_loop(0, MOVE_TILE, start, 0)
    lax.fori_loop(0, MOVE_TILE, wait, 0)
    wts = wts_ref[...]
    moe = wts[:, 0:1] * y0_ref[...] + wts[:, 1:2] * y1_ref[...]
    o_ref[...] = x1_ref[...] + mod_ref[5:6, :] * moe


def _combine(l, ys, pos0, pos1, x1, mod, wts, t_ctx, dec_seq):
    t = x1.shape[0]
    return pl.pallas_call(
        _combine_kernel,
        out_shape=jax.ShapeDtypeStruct((t, D_MODEL), F32),
        grid_spec=pltpu.PrefetchScalarGridSpec(
            num_scalar_prefetch=2,
            grid=(t // MOVE_TILE,),
            in_specs=[
                pl.BlockSpec(memory_space=pl.ANY),
                pl.BlockSpec((MOVE_TILE, D_MODEL), lambda i, p0, p1: (i, 0)),
                pl.BlockSpec((None, None, N_MOD, D_MODEL),
                             lambda i, p0, p1: (l, _cond_of_tile(i, MOVE_TILE, t_ctx, dec_seq), 0, 0)),
                pl.BlockSpec((MOVE_TILE, LANES), lambda i, p0, p1: (i, 0)),
            ],
            out_specs=pl.BlockSpec((MOVE_TILE, D_MODEL), lambda i, p0, p1: (i, 0)),
            scratch_shapes=[pltpu.VMEM((MOVE_TILE, D_MODEL), F32),
                            pltpu.VMEM((MOVE_TILE, D_MODEL), F32),
                            pltpu.SemaphoreType.DMA],
        ),
        compiler_params=pltpu.CompilerParams(dimension_semantics=("arbitrary",),
                                             vmem_limit_bytes=VMEM_LIMIT),
        name="combine",
    )(pos0, pos1, ys, x1, mod, wts)


def _moe(l, h2, x1, mod, router_w, router_b, w1, w3, w2, t_ctx, dec_seq):
    t = h2.shape[0]
    ids, wts, counts = _route(h2, router_w, router_b)
    counts = counts[0, :N_EXPERTS].astype(jnp.int32)
    tiles_per_expert = (counts + FFN_TILE - 1) // FFN_TILE
    tile_end = jnp.cumsum(tiles_per_expert)
    slot_start = (tile_end - tiles_per_expert) * FFN_TILE
    n_tiles = (2 * t) // FFN_TILE + N_EXPERTS
    tile_id = jnp.arange(n_tiles, dtype=jnp.int32)
    tile_expert = jnp.minimum(jnp.sum(tile_id[:, None] >= tile_end[None, :], axis=1), N_EXPERTS - 1)
    tile_valid = (tile_id < tile_end[-1]).astype(jnp.int32)
    pos0 = slot_start[ids[:, 0]] + ids[:, 2]
    pos1 = slot_start[ids[:, 1]] + ids[:, 3]
    xs = _dispatch(h2, pos0, pos1, n_tiles * FFN_TILE)
    ys = _expert_ffn(xs, tile_expert.astype(jnp.int32), tile_valid, w1, w3, w2)
    return _combine(l, ys, pos0, pos1, x1, mod, wts, t_ctx, dec_seq)


def _rope_tables(dec_seq):
    pos = jnp.arange(dec_seq)
    row = (pos // GRID_W).astype(F32)
    col = (pos % GRID_W).astype(F32)
    nf = HEAD_DIM // 4
    inv = ROPE_THETA ** (-jnp.arange(nf, dtype=F32) / nf)
    ang_r = row[:, None] * inv
    ang_c = col[:, None] * inv
    cos = jnp.concatenate([jnp.cos(ang_r)] * 2 + [jnp.cos(ang_c)] * 2, axis=1)
    sin = jnp.concatenate([-jnp.sin(ang_r), jnp.sin(ang_r), -jnp.sin(ang_c), jnp.sin(ang_c)], axis=1)
    cos = jnp.concatenate([jnp.ones((SEQ_TILE, HEAD_DIM), F32), cos], axis=0)
    sin = jnp.concatenate([jnp.zeros((SEQ_TILE, HEAD_DIM), F32), sin], axis=0)
    return jnp.tile(cos, (1, N_HEADS)), jnp.tile(sin, (1, N_HEADS))


def kernel(x_prompt, x_sample, cache_k, cache_v, c, c_ctx, norm1_w, norm2_w, ada_w, ada_b, w_in, w_o,
           q_norm_w, k_norm_w, pool_w, pool_scale, ffn_w1, ffn_w3, ffn_w2, router_w, router_b,
           moe_w1, moe_w3, moe_w2):
    n_ctx, seq, _ = x_prompt.shape
    n_lat, dec_seq, _ = x_sample.shape
    depth = w_in.shape[0]
    past = cache_k.shape[2]
    t_ctx = n_ctx * seq
    assert seq == SEQ_TILE and dec_seq % SEQ_TILE == 0 and t_ctx % dec_seq == 0
    assert t_ctx % FFN_TILE == 0 and dec_seq % FFN_TILE == 0 and 1 + n_lat <= COND_ROWS

    x = jnp.concatenate([x_prompt.reshape(t_ctx, D_MODEL), x_sample.reshape(n_lat * dec_seq, D_MODEL)], axis=0)
    cond = jnp.zeros((COND_ROWS, D_MODEL), F32).at[0].set(c_ctx).at[1:1 + n_lat].set(c)
    mod = _ada(cond, ada_w, ada_b).reshape(depth, COND_ROWS, N_MOD, D_MODEL)

    cos_t, sin_t = _rope_tables(dec_seq)
    head = jnp.arange(ATTN_WIDTH) // HEAD_DIM
    bd = (head[:, None] == head[None, :]).astype(BF16)
    qw = jnp.tile(q_norm_w, (1, N_HEADS)).reshape(depth, 1, ATTN_WIDTH)
    kw = jnp.tile(k_norm_w, (1, N_KV_HEADS)).reshape(depth, 1, KV_WIDTH)
    n1 = norm1_w.reshape(depth, 1, D_MODEL)
    n2 = norm2_w.reshape(depth, 1, D_MODEL)
    ck = cache_k.reshape(n_lat, depth, past, KV_WIDTH)
    cv = cache_v.reshape(n_lat, depth, past, KV_WIDTH)
    w_in_b = w_in.astype(BF16)
    w_o_b = w_o.astype(BF16)
    pool_w_b = pool_w.astype(BF16)
    pool_s = pool_scale.reshape(depth, 1, POOL_WIDTH)

    new_k, new_v = [], []
    for l in range(depth):
        q, k, v, u = _in_proj(l, x, mod, n1, w_in_b, qw, kw, bd, cos_t, sin_t, t_ctx, dec_seq)
        new_k.append(k[:t_ctx].reshape(n_ctx, seq, N_KV_HEADS, HEAD_DIM))
        new_v.append(v[:t_ctx].reshape(n_ctx, seq, N_KV_HEADS, HEAD_DIM))
        attn = _attention(l, q, k, v, ck, cv, n_ctx, seq, n_lat, dec_seq)
        pool = _pool(l, u, pool_w_b, pool_s, n_ctx, seq, n_lat, dec_seq)
        if l % 2 == 0:
            x1, h2 = _out_proj(l, attn, pool, x, mod, n2, w_o_b, BF16, t_ctx, dec_seq)
            x = _dense_ffn(l, h2, x1, mod, ffn_w1.astype(BF16), ffn_w3.astype(BF16), ffn_w2.astype(BF16),
                           t_ctx, dec_seq)
        else:
            x1, h2 = _out_proj(l, attn, pool, x, mod, n2, w_o_b, F32, t_ctx, dec_seq)
            x = _moe(l, h2, x1, mod, router_w[l // 2], router_b[l // 2],
                     moe_w1[l // 2].astype(BF16), moe_w3[l // 2].astype(BF16), moe_w2[l // 2].astype(BF16),
                     t_ctx, dec_seq)

    y_prompt = x[:t_ctx].reshape(n_ctx, seq, D_MODEL)
    y_sample = x[t_ctx:].reshape(n_lat, dec_seq, D_MODEL)
    return y_prompt, y_sample, jnp.stack(new_k, axis=1), jnp.stack(new_v, axis=1)
```

```python
import functools

import jax
import jax.numpy as jnp
from jax import lax
from jax.experimental import pallas as pl
from jax.experimental.pallas import tpu as pltpu

F32 = jnp.float32
BF16 = jnp.bfloat16

D_MODEL = 1024
HEAD_DIM = 64
N_HEADS = 8
N_KV_HEADS = 2
ATTN_WIDTH = N_HEADS * HEAD_DIM
KV_WIDTH = N_KV_HEADS * HEAD_DIM
KVX_WIDTH = 2 * N_KV_HEADS * 2 * HEAD_DIM
POOL_WIDTH = D_MODEL - ATTN_WIDTH
N_POOL_GROUPS = 4
POOL_GC = POOL_WIDTH // N_POOL_GROUPS
IN_COLS = ATTN_WIDTH + 2 * KV_WIDTH + POOL_WIDTH
D_FF = 2816
N_EXPERTS = 8
GRID_W = 64
ROPE_THETA = 10000.0
EPS = 1e-6
N_MOD = 6

LANES = 128
SEQ_TILE = 256
FFN_TILE = 512
FF_CHUNK = 1408
ROUTE_TILE = 512
MOVE_TILE = 256
COND_ROWS = 16
VMEM_LIMIT = 48 * 1024 * 1024
NEG_BIG = -1e30


def _cond_of_tile(i, tile, t_ctx, dec_seq):
    tok = i * tile
    return jnp.where(tok < t_ctx, 0, 1 + (tok - t_ctx) // dec_seq)


def _dot(a, b):
    return jnp.dot(a, b, preferred_element_type=F32)


def _dot_nt(a, b):
    return lax.dot_general(a, b, (((1,), (1,)), ((), ())), preferred_element_type=F32)


def _split_bf16(x):
    hi = x.astype(BF16)
    lo = (x - hi.astype(F32)).astype(BF16)
    return hi, lo


def _rms_mod(x, nw, scale, shift):
    ms = jnp.mean(x * x, axis=-1, keepdims=True)
    y = x * lax.rsqrt(ms + EPS) * nw
    return y * (1.0 + scale) + shift


def _ada_kernel(c_ref, w_ref, b_ref, o_ref):
    c = c_ref[...]
    s = c * jax.nn.sigmoid(c)
    hi, lo = _split_bf16(s)
    w = w_ref[...]
    whi, wlo = _split_bf16(w)
    o_ref[...] = _dot(hi, whi) + _dot(lo, whi) + _dot(hi, wlo) + b_ref[...]


def _ada(cond, ada_w, ada_b):
    depth = ada_w.shape[0]
    cols = ada_w.shape[2]
    tn = cols // 4
    return pl.pallas_call(
        _ada_kernel,
        out_shape=jax.ShapeDtypeStruct((depth, COND_ROWS, cols), F32),
        grid=(depth, cols // tn),
        in_specs=[
            pl.BlockSpec((COND_ROWS, D_MODEL), lambda l, n: (0, 0)),
            pl.BlockSpec((None, D_MODEL, tn), lambda l, n: (l, 0, n)),
            pl.BlockSpec((None, 1, tn), lambda l, n: (l, 0, n)),
        ],
        out_specs=pl.BlockSpec((None, COND_ROWS, tn), lambda l, n: (l, 0, n)),
        compiler_params=pltpu.CompilerParams(vmem_limit_bytes=VMEM_LIMIT),
        name="ada_mod",
    )(cond, ada_w, ada_b.reshape(depth, 1, cols))


def _head_rms(t, bd, w):
    hi, lo = _split_bf16(t * t)
    ss = _dot(hi, bd) + _dot(lo, bd)
    return t * lax.rsqrt(ss * (1.0 / HEAD_DIM) + EPS) * w


def _rope(t, cos, sin):
    width = t.shape[1]
    lane = lax.broadcasted_iota(jnp.int32, t.shape, 1)
    first = (lane % 32) < 16
    partner = jnp.where(first, pltpu.roll(t, width - 16, 1), pltpu.roll(t, 16, 1))
    return t * cos + partner * sin


def _spread_kv(t):
    lane = lax.broadcasted_iota(jnp.int32, t.shape, 1)
    low0 = jnp.where(lane < HEAD_DIM, t, 0.0)
    high1 = jnp.where(lane >= HEAD_DIM, t, 0.0)
    parts = [low0, pltpu.roll(low0, HEAD_DIM, 1), pltpu.roll(high1, HEAD_DIM, 1), high1]
    return jnp.concatenate(parts, axis=1).astype(BF16)


def _in_kernel(x_ref, mod_ref, nw_ref, w_ref, qw_ref, kw_ref, bd_ref, cos_ref, sin_ref,
               q_ref, kx_ref, vx_ref, k_ref, v_ref, u_ref):
    h = _rms_mod(x_ref[...], nw_ref[...], mod_ref[1:2, :], mod_ref[0:1, :])
    proj = _dot(h.astype(BF16), w_ref[...])
    q = proj[:, :ATTN_WIDTH]
    k = proj[:, ATTN_WIDTH:ATTN_WIDTH + KV_WIDTH]
    v = proj[:, ATTN_WIDTH + KV_WIDTH:ATTN_WIDTH + 2 * KV_WIDTH]
    u_ref[...] = proj[:, ATTN_WIDTH + 2 * KV_WIDTH:]
    cos = cos_ref[...]
    sin = sin_ref[...]
    q = _head_rms(q, bd_ref[...], qw_ref[...])
    k = _head_rms(k, bd_ref[:KV_WIDTH, :KV_WIDTH], kw_ref[...])
    q_ref[...] = (_rope(q, cos, sin) * (HEAD_DIM ** -0.5)).astype(BF16)
    k = _rope(k, cos[:, :KV_WIDTH], sin[:, :KV_WIDTH])
    k_ref[...] = k
    v_ref[...] = v
    kx_ref[...] = _spread_kv(k)
    vx_ref[...] = _spread_kv(v)


def _in_proj(l, x, mod, norm_w, w_in, qw, kw, bd, cos_t, sin_t, t_ctx, dec_seq):
    t = x.shape[0]
    n_rope = dec_seq // SEQ_TILE

    def rope_idx(i):
        return (jnp.where(i * SEQ_TILE < t_ctx, 0, 1 + (i - t_ctx // SEQ_TILE) % n_rope), 0)

    row = lambda i: (i, 0)
    return pl.pallas_call(
        _in_kernel,
        out_shape=(jax.ShapeDtypeStruct((t, ATTN_WIDTH), BF16),
                   jax.ShapeDtypeStruct((t, KVX_WIDTH), BF16),
                   jax.ShapeDtypeStruct((t, KVX_WIDTH), BF16),
                   jax.ShapeDtypeStruct((t, KV_WIDTH), F32),
                   jax.ShapeDtypeStruct((t, KV_WIDTH), F32),
                   jax.ShapeDtypeStruct((t, POOL_WIDTH), F32)),
        grid=(t // SEQ_TILE,),
        in_specs=[
            pl.BlockSpec((SEQ_TILE, D_MODEL), row),
            pl.BlockSpec((None, None, N_MOD, D_MODEL),
                         lambda i: (l, _cond_of_tile(i, SEQ_TILE, t_ctx, dec_seq), 0, 0)),
            pl.BlockSpec((None, 1, D_MODEL), lambda i: (l, 0, 0)),
            pl.BlockSpec((None, D_MODEL, IN_COLS), lambda i: (l, 0, 0)),
            pl.BlockSpec((None, 1, ATTN_WIDTH), lambda i: (l, 0, 0)),
            pl.BlockSpec((None, 1, KV_WIDTH), lambda i: (l, 0, 0)),
            pl.BlockSpec((ATTN_WIDTH, ATTN_WIDTH), lambda i: (0, 0)),
            pl.BlockSpec((SEQ_TILE, ATTN_WIDTH), rope_idx),
            pl.BlockSpec((SEQ_TILE, ATTN_WIDTH), rope_idx),
        ],
        out_specs=(pl.BlockSpec((SEQ_TILE, ATTN_WIDTH), row),
                   pl.BlockSpec((SEQ_TILE, KVX_WIDTH), row),
                   pl.BlockSpec((SEQ_TILE, KVX_WIDTH), row),
                   pl.BlockSpec((SEQ_TILE, KV_WIDTH), row),
                   pl.BlockSpec((SEQ_TILE, KV_WIDTH), row),
                   pl.BlockSpec((SEQ_TILE, POOL_WIDTH), row)),
        compiler_params=pltpu.CompilerParams(vmem_limit_bytes=VMEM_LIMIT),
        name=f"in_proj_{l}",
    )(x, mod, norm_w, w_in, qw, kw, bd, cos_t, sin_t)


def _attend(q_ref, key_refs, val_refs, mix_ref):
    heads_per_kv = N_HEADS // N_KV_HEADS
    for pair in range(ATTN_WIDTH // LANES):
        kv = (2 * pair) // heads_per_kv
        qb = q_ref[:, pair * LANES:(pair + 1) * LANES]
        out = None
        for half in range(2):
            c0 = (2 * kv + half) * LANES
            scores = [_dot_nt(qb, k[:, c0:c0 + LANES]) for k in key_refs]
            m = functools.reduce(jnp.maximum, [jnp.max(s, axis=-1, keepdims=True) for s in scores])
            ps = [jnp.exp(s - m) for s in scores]
            denom = functools.reduce(jnp.add, [jnp.sum(p, axis=-1, keepdims=True) for p in ps])
            o = functools.reduce(jnp.add, [_dot(p.astype(BF16), v[:, c0:c0 + LANES])
                                           for p, v in zip(ps, val_refs)])
            o = o / denom
            out = o if out is None else out + o
        mix_ref[:, pair * LANES:(pair + 1) * LANES] = out.astype(BF16)


def _pool_tile(u_ref, row0, slab0, slab_rows, seq_len, pw_ref, ps_ref, mix_ref):
    rows = SEQ_TILE
    r = row0 + lax.broadcasted_iota(jnp.int32, (rows, slab_rows), 0)
    c = slab0 + lax.broadcasted_iota(jnp.int32, (rows, slab_rows), 1)
    offset = c - r
    pos = row0 + lax.broadcasted_iota(jnp.int32, (rows, POOL_GC), 0)
    for g in range(N_POOL_GROUPS):
        half = 1 << g
        lanes = slice(g * POOL_GC, (g + 1) * POOL_GC)
        band = jnp.where(offset >= -half, jnp.where(offset < half, 1.0, 0.0), 0.0).astype(BF16)
        hi, lo = _split_bf16(u_ref[pl.ds(slab0, slab_rows), lanes])
        window_sum = _dot(band, hi) + _dot(band, lo)
        count = jnp.minimum(pos + half, seq_len) - jnp.maximum(pos - half, 0)
        d = window_sum / count.astype(F32) - u_ref[pl.ds(row0, rows), lanes]
        y = _dot(d.astype(BF16), pw_ref[g]) * ps_ref[:, lanes]
        mix_ref[:, ATTN_WIDTH + g * POOL_GC:ATTN_WIDTH + (g + 1) * POOL_GC] = y.astype(BF16)


def _mixer_kernel(q_ref, kc_ref, vc_ref, kl_ref, vl_ref, ck_ref, cv_ref, uc_ref, ul_ref,
                  x_ref, mod_ref, nw_ref, wo_ref, pw_ref, ps_ref, x1_ref, h2_ref, mix_ref,
                  *, ctx_tiles, lat_tiles):
    i = pl.program_id(0)

    @pl.when(i < ctx_tiles)
    def _():
        _attend(q_ref, [kc_ref], [vc_ref], mix_ref)
        _pool_tile(uc_ref, 0, 0, SEQ_TILE, SEQ_TILE, pw_ref, ps_ref, mix_ref)

    @pl.when(i >= ctx_tiles)
    def _():
        _attend(q_ref, [kl_ref, ck_ref], [vl_ref, cv_ref], mix_ref)
        seq_len = lat_tiles * SEQ_TILE
        row0 = pl.multiple_of(((i - ctx_tiles) % lat_tiles) * SEQ_TILE, SEQ_TILE)
        slab_rows = 2 * SEQ_TILE
        slab0 = pl.multiple_of(jnp.clip(row0 - SEQ_TILE // 2, 0, seq_len - slab_rows), SEQ_TILE // 2)
        _pool_tile(ul_ref, row0, slab0, slab_rows, seq_len, pw_ref, ps_ref, mix_ref)

    x1 = x_ref[...] + mod_ref[2:3, :] * _dot(mix_ref[...], wo_ref[...])
    x1_ref[...] = x1
    h2_ref[...] = _rms_mod(x1, nw_ref[...], mod_ref[4:5, :], mod_ref[3:4, :]).astype(h2_ref.dtype)


def _mixer(l, q, kx, vx, ckx, cvx, u, x, mod, norm_w, w_o, pool_w, pool_scale, h2_dtype, t_ctx, dec_seq):
    t = x.shape[0]
    past = ckx.shape[2]
    ctx_tiles = t_ctx // SEQ_TILE
    lat_tiles = dec_seq // SEQ_TILE
    lat_seq0 = t_ctx // dec_seq
    row = lambda i: (i, 0)
    ctx_row = lambda i: (jnp.minimum(i, ctx_tiles - 1), 0)
    lat_seq = lambda i: (lat_seq0 + jnp.maximum(i - ctx_tiles, 0) // lat_tiles, 0)
    cache = lambda i: (jnp.maximum(i - ctx_tiles, 0) // lat_tiles, l, 0, 0)
    return pl.pallas_call(
        functools.partial(_mixer_kernel, ctx_tiles=ctx_tiles, lat_tiles=lat_tiles),
        out_shape=(jax.ShapeDtypeStruct((t, D_MODEL), F32),
                   jax.ShapeDtypeStruct((t, D_MODEL), h2_dtype)),
        grid=(t // SEQ_TILE,),
        in_specs=[
            pl.BlockSpec((SEQ_TILE, ATTN_WIDTH), row),
            pl.BlockSpec((SEQ_TILE, KVX_WIDTH), ctx_row),
            pl.BlockSpec((SEQ_TILE, KVX_WIDTH), ctx_row),
            pl.BlockSpec((dec_seq, KVX_WIDTH), lat_seq),
            pl.BlockSpec((dec_seq, KVX_WIDTH), lat_seq),
            pl.BlockSpec((None, None, past, KVX_WIDTH), cache),
            pl.BlockSpec((None, None, past, KVX_WIDTH), cache),
            pl.BlockSpec((SEQ_TILE, POOL_WIDTH), ctx_row),
            pl.BlockSpec((dec_seq, POOL_WIDTH), lat_seq),
            pl.BlockSpec((SEQ_TILE, D_MODEL), row),
            pl.BlockSpec((None, None, N_MOD, D_MODEL),
                         lambda i: (l, _cond_of_tile(i, SEQ_TILE, t_ctx, dec_seq), 0, 0)),
            pl.BlockSpec((None, 1, D_MODEL), lambda i: (l, 0, 0)),
            pl.BlockSpec((None, D_MODEL, D_MODEL), lambda i: (l, 0, 0)),
            pl.BlockSpec((None, N_POOL_GROUPS, POOL_GC, POOL_GC), lambda i: (l, 0, 0, 0)),
            pl.BlockSpec((None, 1, POOL_WIDTH), lambda i: (l, 0, 0)),
        ],
        out_specs=(pl.BlockSpec((SEQ_TILE, D_MODEL), row),
                   pl.BlockSpec((SEQ_TILE, D_MODEL), row)),
        scratch_shapes=[pltpu.VMEM((SEQ_TILE, D_MODEL), BF16)],
        compiler_params=pltpu.CompilerParams(vmem_limit_bytes=VMEM_LIMIT),
        name=f"mixer_{l}",
    )(q, kx, vx, kx, vx, ckx, cvx, u, u, x, mod, norm_w, w_o, pool_w, pool_scale)


def _swiglu_step(h, w1_ref, w3_ref, w2_ref, acc_ref):
    a = _dot(h, w1_ref[...])
    b = _dot(h, w3_ref[...])
    gated = (a * jax.nn.sigmoid(a) * b).astype(BF16)
    acc_ref[...] += _dot(gated, w2_ref[...])


def _ffn_kernel(h_ref, w1_ref, w3_ref, w2_ref, x1_ref, mod_ref, o_ref, acc_ref):
    f = pl.program_id(1)

    @pl.when(f == 0)
    def _():
        acc_ref[...] = jnp.zeros_like(acc_ref)

    _swiglu_step(h_ref[...], w1_ref, w3_ref, w2_ref, acc_ref)

    @pl.when(f == pl.num_programs(1) - 1)
    def _():
        o_ref[...] = x1_ref[...] + mod_ref[5:6, :] * acc_ref[...]


def _dense_ffn(l, h2, x1, mod, w1, w3, w2, t_ctx, dec_seq):
    t = x1.shape[0]
    row = lambda i, f: (i, 0)
    return pl.pallas_call(
        _ffn_kernel,
        out_shape=jax.ShapeDtypeStruct((t, D_MODEL), F32),
        grid=(t // FFN_TILE, D_FF // FF_CHUNK),
        in_specs=[
            pl.BlockSpec((FFN_TILE, D_MODEL), row),
            pl.BlockSpec((None, D_MODEL, FF_CHUNK), lambda i, f: (l // 2, 0, f)),
            pl.BlockSpec((None, D_MODEL, FF_CHUNK), lambda i, f: (l // 2, 0, f)),
            pl.BlockSpec((None, FF_CHUNK, D_MODEL), lambda i, f: (l // 2, f, 0)),
            pl.BlockSpec((FFN_TILE, D_MODEL), row),
            pl.BlockSpec((None, None, N_MOD, D_MODEL),
                         lambda i, f: (l, _cond_of_tile(i, FFN_TILE, t_ctx, dec_seq), 0, 0)),
        ],
        out_specs=pl.BlockSpec((FFN_TILE, D_MODEL), row),
        scratch_shapes=[pltpu.VMEM((FFN_TILE, D_MODEL), F32)],
        compiler_params=pltpu.CompilerParams(vmem_limit_bytes=VMEM_LIMIT),
        name=f"dense_ffn_{l}",
    )(h2, w1, w3, w2, x1, mod)


def _route_kernel(h_ref, rw_ref, rb_ref, ids_ref, wts_ref, cnt_ref, carry_ref):
    @pl.when(pl.program_id(0) == 0)
    def _():
        carry_ref[...] = jnp.zeros_like(carry_ref)

    h = h_ref[...]
    hi, lo = _split_bf16(h)
    rw = rw_ref[...]
    whi, wlo = _split_bf16(rw)
    logits = _dot(hi, whi) + _dot(lo, whi) + _dot(hi, wlo) + rb_ref[...]
    lane = lax.broadcasted_iota(jnp.int32, logits.shape, 1)
    lane_f = lane.astype(F32)
    m0 = jnp.max(logits, axis=-1, keepdims=True)
    e0 = jnp.min(jnp.where(logits == m0, lane_f, float(LANES)), axis=-1, keepdims=True)
    rest = jnp.where(lane_f == e0, -jnp.inf, logits)
    m1 = jnp.max(rest, axis=-1, keepdims=True)
    e1 = jnp.min(jnp.where(rest == m1, lane_f, float(LANES)), axis=-1, keepdims=True)
    ex = jnp.exp(m1 - m0)
    w0 = 1.0 / (1.0 + ex)
    w1 = ex / (1.0 + ex)

    n = logits.shape[0]
    sel0 = lane_f == e0
    sel1 = lane_f == e1
    mask = jnp.where(sel0 | sel1, 1.0, 0.0)
    r = lax.broadcasted_iota(jnp.int32, (n, n), 0)
    c = lax.broadcasted_iota(jnp.int32, (n, n), 1)
    below = jnp.where(c < r, 1.0, 0.0).astype(BF16)
    rank = _dot(below, mask.astype(BF16)) + carry_ref[0:1, :]
    r0 = jnp.sum(jnp.where(sel0, rank, 0.0), axis=-1, keepdims=True).astype(jnp.int32)
    r1 = jnp.sum(jnp.where(sel1, rank, 0.0), axis=-1, keepdims=True).astype(jnp.int32)
    carry_ref[...] = carry_ref[...] + jnp.sum(mask, axis=0, keepdims=True)
    cnt_ref[...] = carry_ref[...]

    ids_ref[...] = jnp.where(lane == 0, e0.astype(jnp.int32), jnp.where(lane == 1, e1.astype(jnp.int32),
                             jnp.where(lane == 2, r0, jnp.where(lane == 3, r1, 0))))
    wts_ref[...] = jnp.where(lane == 0, w0, jnp.where(lane == 1, w1, 0.0))


def _route(h2, router_w, router_b):
    t = h2.shape[0]
    rw = jnp.zeros((D_MODEL, LANES), F32).at[:, :N_EXPERTS].set(router_w)
    rb = jnp.full((1, LANES), NEG_BIG, F32).at[0, :N_EXPERTS].set(router_b)
    row = lambda i: (i, 0)
    return pl.pallas_call(
        _route_kernel,
        out_shape=(jax.ShapeDtypeStruct((t, LANES), jnp.int32),
                   jax.ShapeDtypeStruct((t, LANES), F32),
                   jax.ShapeDtypeStruct((8, LANES), F32)),
        grid=(t // ROUTE_TILE,),
        in_specs=[
            pl.BlockSpec((ROUTE_TILE, D_MODEL), row),
            pl.BlockSpec((D_MODEL, LANES), lambda i: (0, 0)),
            pl.BlockSpec((1, LANES), lambda i: (0, 0)),
        ],
        out_specs=(pl.BlockSpec((ROUTE_TILE, LANES), row),
                   pl.BlockSpec((ROUTE_TILE, LANES), row),
                   pl.BlockSpec((8, LANES), lambda i: (0, 0))),
        scratch_shapes=[pltpu.VMEM((8, LANES), F32)],
        compiler_params=pltpu.CompilerParams(dimension_semantics=("arbitrary",),
                                             vmem_limit_bytes=VMEM_LIMIT),
        name="route",
    )(h2, rw, rb)


def _row_copy(src_ref, src_row, dst_ref, dst_row, sem):
    return pltpu.make_async_copy(src_ref.at[pl.ds(src_row, 1)], dst_ref.at[pl.ds(dst_row, 1)], sem)


def _dispatch_kernel(p0_ref, p1_ref, h_ref, init_ref, xs_ref, sem):
    del init_ref
    base = pl.program_id(0) * MOVE_TILE

    def copies(r):
        return (_row_copy(h_ref, r, xs_ref, p0_ref[base + r], sem),
                _row_copy(h_ref, r, xs_ref, p1_ref[base + r], sem))

    def start(r, carry):
        for cp in copies(r):
            cp.start()
        return carry

    def wait(r, carry):
        for cp in copies(r):
            cp.wait()
        return carry

    lax.fori_loop(0, MOVE_TILE, start, 0)
    lax.fori_loop(0, MOVE_TILE, wait, 0)


def _dispatch(h2, pos0, pos1, n_slots):
    t = h2.shape[0]
    xs_init = jnp.zeros((n_slots, D_MODEL), F32)
    return pl.pallas_call(
        _dispatch_kernel,
        out_shape=jax.ShapeDtypeStruct((n_slots, D_MODEL), F32),
        grid_spec=pltpu.PrefetchScalarGridSpec(
            num_scalar_prefetch=2,
            grid=(t // MOVE_TILE,),
            in_specs=[pl.BlockSpec((MOVE_TILE, D_MODEL), lambda i, p0, p1: (i, 0)),
                      pl.BlockSpec(memory_space=pl.ANY)],
            out_specs=pl.BlockSpec(memory_space=pl.ANY),
            scratch_shapes=[pltpu.SemaphoreType.DMA],
        ),
        input_output_aliases={3: 0},
        compiler_params=pltpu.CompilerParams(dimension_semantics=("arbitrary",),
                                             vmem_limit_bytes=VMEM_LIMIT),
        name="dispatch",
    )(pos0, pos1, h2, xs_init)


def _expert_kernel(te_ref, tv_ref, x_ref, w1_ref, w3_ref, w2_ref, o_ref, acc_ref):
    i = pl.program_id(0)
    f = pl.program_id(1)
    last = pl.num_programs(1) - 1

    @pl.when(f == 0)
    def _():
        acc_ref[...] = jnp.zeros_like(acc_ref)

    @pl.when(tv_ref[i] == 1)
    def _():
        _swiglu_step(x_ref[...].astype(BF16), w1_ref, w3_ref, w2_ref, acc_ref)

    @pl.when(f == last)
    def _():
        o_ref[...] = acc_ref[...]


def _expert_ffn(xs, tile_expert, tile_valid, w1, w3, w2):
    n_slots = xs.shape[0]
    row = lambda i, f, te, tv: (i, 0)
    return pl.pallas_call(
        _expert_kernel,
        out_shape=jax.ShapeDtypeStruct((n_slots, D_MODEL), F32),
        grid_spec=pltpu.PrefetchScalarGridSpec(
            num_scalar_prefetch=2,
            grid=(n_slots // FFN_TILE, D_FF // FF_CHUNK),
            in_specs=[
                pl.BlockSpec((FFN_TILE, D_MODEL), row),
                pl.BlockSpec((None, D_MODEL, FF_CHUNK), lambda i, f, te, tv: (te[i], 0, f)),
                pl.BlockSpec((None, D_MODEL, FF_CHUNK), lambda i, f, te, tv: (te[i], 0, f)),
                pl.BlockSpec((None, FF_CHUNK, D_MODEL), lambda i, f, te, tv: (te[i], f, 0)),
            ],
            out_specs=pl.BlockSpec((FFN_TILE, D_MODEL), row),
            scratch_shapes=[pltpu.VMEM((FFN_TILE, D_MODEL), F32)],
        ),
        compiler_params=pltpu.CompilerParams(vmem_limit_bytes=VMEM_LIMIT),
        name="expert_ffn",
    )(tile_expert, tile_valid, xs, w1, w3, w2)


def _combine_kernel(p0_ref, p1_ref, ys_ref, x1_ref, mod_ref, wts_ref, o_ref, y0_ref, y1_ref, sem):
    base = pl.program_id(0) * MOVE_TILE

    def copies(r):
        return (_row_copy(ys_ref, p0_ref[base + r], y0_ref, r, sem),
                _row_copy(ys_ref, p1_ref[base + r], y1_ref, r, sem))

    def start(r, carry):
        for cp in copies(r):
            cp.start()
        return carry

    def wait(r, carry):
        for cp in copies(r):
            cp.wait()
        return carry

    lax.fori_loop(0, MOVE_TILE, start, 0)
    lax.fori_loop(0, MOVE_TILE, wait, 0)
    wts = wts_ref[...]
    moe = wts[:, 0:1] * y0_ref[...] + wts[:, 1:2] * y1_ref[...]
    o_ref[...] = x1_ref[...] + mod_ref[5:6, :] * moe


def _combine(l, ys, pos0, pos1, x1, mod, wts, t_ctx, dec_seq):
    t = x1.shape[0]
    return pl.pallas_call(
        _combine_kernel,
        out_shape=jax.ShapeDtypeStruct((t, D_MODEL), F32),
        grid_spec=pltpu.PrefetchScalarGridSpec(
            num_scalar_prefetch=2,
            grid=(t // MOVE_TILE,),
            in_specs=[
                pl.BlockSpec(memory_space=pl.ANY),
                pl.BlockSpec((MOVE_TILE, D_MODEL), lambda i, p0, p1: (i, 0)),
                pl.BlockSpec((None, None, N_MOD, D_MODEL),
                             lambda i, p0, p1: (l, _cond_of_tile(i, MOVE_TILE, t_ctx, dec_seq), 0, 0)),
                pl.BlockSpec((MOVE_TILE, LANES), lambda i, p0, p1: (i, 0)),
            ],
            out_specs=pl.BlockSpec((MOVE_TILE, D_MODEL), lambda i, p0, p1: (i, 0)),
            scratch_shapes=[pltpu.VMEM((MOVE_TILE, D_MODEL), F32),
                            pltpu.VMEM((MOVE_TILE, D_MODEL), F32),
                            pltpu.SemaphoreType.DMA],
        ),
        compiler_params=pltpu.CompilerParams(dimension_semantics=("arbitrary",),
                                             vmem_limit_bytes=VMEM_LIMIT),
        name="combine",
    )(pos0, pos1, ys, x1, mod, wts)


def _moe(l, h2, x1, mod, router_w, router_b, w1, w3, w2, t_ctx, dec_seq):
    t = h2.shape[0]
    ids, wts, counts = _route(h2, router_w, router_b)
    counts = counts[0, :N_EXPERTS].astype(jnp.int32)
    tiles_per_expert = (counts + FFN_TILE - 1) // FFN_TILE
    tile_end = jnp.cumsum(tiles_per_expert)
    slot_start = (tile_end - tiles_per_expert) * FFN_TILE
    n_tiles = (2 * t) // FFN_TILE + N_EXPERTS
    tile_id = jnp.arange(n_tiles, dtype=jnp.int32)
    tile_expert = jnp.minimum(jnp.sum(tile_id[:, None] >= tile_end[None, :], axis=1), N_EXPERTS - 1)
    tile_valid = (tile_id < tile_end[-1]).astype(jnp.int32)
    pos0 = slot_start[ids[:, 0]] + ids[:, 2]
    pos1 = slot_start[ids[:, 1]] + ids[:, 3]
    xs = _dispatch(h2, pos0, pos1, n_tiles * FFN_TILE)
    ys = _expert_ffn(xs, tile_expert.astype(jnp.int32), tile_valid, w1, w3, w2)
    return _combine(l, ys, pos0, pos1, x1, mod, wts, t_ctx, dec_seq)


def _rope_tables(dec_seq):
    pos = jnp.arange(dec_seq)
    row = (pos // GRID_W).astype(F32)
    col = (pos % GRID_W).astype(F32)
    nf = HEAD_DIM // 4
    inv = ROPE_THETA ** (-jnp.arange(nf, dtype=F32) / nf)
    ang_r = row[:, None] * inv
    ang_c = col[:, None] * inv
    cos = jnp.concatenate([jnp.cos(ang_r)] * 2 + [jnp.cos(ang_c)] * 2, axis=1)
    sin = jnp.concatenate([-jnp.sin(ang_r), jnp.sin(ang_r), -jnp.sin(ang_c), jnp.sin(ang_c)], axis=1)
    cos = jnp.concatenate([jnp.ones((SEQ_TILE, HEAD_DIM), F32), cos], axis=0)
    sin = jnp.concatenate([jnp.zeros((SEQ_TILE, HEAD_DIM), F32), sin], axis=0)
    return jnp.tile(cos, (1, N_HEADS)), jnp.tile(sin, (1, N_HEADS))


def _spread_cache(cache):
    h0, h1 = cache[..., 0, :], cache[..., 1, :]
    z = jnp.zeros_like(h0)
    return jnp.concatenate([h0, z, z, h0, h1, z, z, h1], axis=-1).astype(BF16)


def kernel(x_prompt, x_sample, cache_k, cache_v, c, c_ctx, norm1_w, norm2_w, ada_w, ada_b, w_in, w_o,
           q_norm_w, k_norm_w, pool_w, pool_scale, ffn_w1, ffn_w3, ffn_w2, router_w, router_b,
           moe_w1, moe_w3, moe_w2):
    n_ctx, seq, _ = x_prompt.shape
    n_lat, dec_seq, _ = x_sample.shape
    depth = w_in.shape[0]
    past = cache_k.shape[2]
    t_ctx = n_ctx * seq
    assert seq == SEQ_TILE and dec_seq % SEQ_TILE == 0 and t_ctx % dec_seq == 0
    assert t_ctx % FFN_TILE == 0 and dec_seq % FFN_TILE == 0 and 1 + n_lat <= COND_ROWS

    x = jnp.concatenate([x_prompt.reshape(t_ctx, D_MODEL), x_sample.reshape(n_lat * dec_seq, D_MODEL)], axis=0)
    cond = jnp.zeros((COND_ROWS, D_MODEL), F32).at[0].set(c_ctx).at[1:1 + n_lat].set(c)
    mod = _ada(cond, ada_w, ada_b).reshape(depth, COND_ROWS, N_MOD, D_MODEL)

    cos_t, sin_t = _rope_tables(dec_seq)
    head = jnp.arange(ATTN_WIDTH) // HEAD_DIM
    bd = (head[:, None] == head[None, :]).astype(BF16)
    qw = jnp.tile(q_norm_w, (1, N_HEADS)).reshape(depth, 1, ATTN_WIDTH)
    kw = jnp.tile(k_norm_w, (1, N_KV_HEADS)).reshape(depth, 1, KV_WIDTH)
    n1 = norm1_w.reshape(depth, 1, D_MODEL)
    n2 = norm2_w.reshape(depth, 1, D_MODEL)
    ckx = _spread_cache(cache_k)
    cvx = _spread_cache(cache_v)
    w_in_b = w_in.astype(BF16)
    w_o_b = w_o.astype(BF16)
    pool_w_b = pool_w.astype(BF16)
    pool_s = pool_scale.reshape(depth, 1, POOL_WIDTH)

    new_k, new_v = [], []
    for l in range(depth):
        q, kx, vx, k, v, u = _in_proj(l, x, mod, n1, w_in_b, qw, kw, bd, cos_t, sin_t, t_ctx, dec_seq)
        new_k.append(k[:t_ctx].reshape(n_ctx, seq, N_KV_HEADS, HEAD_DIM))
        new_v.append(v[:t_ctx].reshape(n_ctx, seq, N_KV_HEADS, HEAD_DIM))
        h2_dtype = BF16 if l % 2 == 0 else F32
        x1, h2 = _mixer(l, q, kx, vx, ckx, cvx, u, x, mod, n2, w_o_b, pool_w_b, pool_s, h2_dtype, t_ctx, dec_seq)
        if l % 2 == 0:
            x = _dense_ffn(l, h2, x1, mod, ffn_w1.astype(BF16), ffn_w3.astype(BF16), ffn_w2.astype(BF16),
                           t_ctx, dec_seq)
        else:
            x = _moe(l, h2, x1, mod, router_w[l // 2], router_b[l // 2],
                     moe_w1[l // 2].astype(BF16), moe_w3[l // 2].astype(BF16), moe_w2[l // 2].astype(BF16),
                     t_ctx, dec_seq)

    y_prompt = x[:t_ctx].reshape(n_ctx, seq, D_MODEL)
    y_sample = x[t_ctx:].reshape(n_lat, dec_seq, D_MODEL)
    return y_prompt, y_sample, jnp.stack(new_k, axis=1), jnp.stack(new_v, axis=1)
```

```python
import functools

import jax
import jax.numpy as jnp
from jax import lax
from jax.experimental import pallas as pl
from jax.experimental.pallas import tpu as pltpu

F32 = jnp.float32
BF16 = jnp.bfloat16

D_MODEL = 1024
HEAD_DIM = 64
N_HEADS = 8
N_KV_HEADS = 2
ATTN_WIDTH = N_HEADS * HEAD_DIM
KV_WIDTH = N_KV_HEADS * HEAD_DIM
KVX_WIDTH = 2 * N_KV_HEADS * 2 * HEAD_DIM
POOL_WIDTH = D_MODEL - ATTN_WIDTH
N_POOL_GROUPS = 4
POOL_GC = POOL_WIDTH // N_POOL_GROUPS
IN_COLS = ATTN_WIDTH + 2 * KV_WIDTH + POOL_WIDTH
D_FF = 2816
N_EXPERTS = 8
GRID_W = 64
ROPE_THETA = 10000.0
EPS = 1e-6
N_MOD = 6

LANES = 128
SEQ_TILE = 256
FFN_TILE = 512
FF_CHUNK = 1408
ROUTE_TILE = 512
META_ROWS = 8
MOVE_TILE = 256
ROW_UNROLL = 8
COND_ROWS = 16
VMEM_LIMIT = 48 * 1024 * 1024
NEG_BIG = -1e30


def _cond_of_tile(i, tile, t_ctx, dec_seq):
    tok = i * tile
    return jnp.where(tok < t_ctx, 0, 1 + (tok - t_ctx) // dec_seq)


def _dot(a, b):
    return jnp.dot(a, b, preferred_element_type=F32)


def _dot_nt(a, b):
    return lax.dot_general(a, b, (((1,), (1,)), ((), ())), preferred_element_type=F32)


def _split_bf16(x):
    hi = x.astype(BF16)
    lo = (x - hi.astype(F32)).astype(BF16)
    return hi, lo


def _rms_mod(x, nw, scale, shift):
    ms = jnp.mean(x * x, axis=-1, keepdims=True)
    y = x * lax.rsqrt(ms + EPS) * nw
    return y * (1.0 + scale) + shift


def _ada_kernel(c_ref, w_ref, b_ref, o_ref):
    c = c_ref[...]
    s = c * jax.nn.sigmoid(c)
    hi, lo = _split_bf16(s)
    w = w_ref[...]
    whi, wlo = _split_bf16(w)
    o_ref[...] = _dot(hi, whi) + _dot(lo, whi) + _dot(hi, wlo) + b_ref[...]


def _ada(cond, ada_w, ada_b):
    depth = ada_w.shape[0]
    cols = ada_w.shape[2]
    tn = cols // 4
    return pl.pallas_call(
        _ada_kernel,
        out_shape=jax.ShapeDtypeStruct((depth, COND_ROWS, cols), F32),
        grid=(depth, cols // tn),
        in_specs=[
            pl.BlockSpec((COND_ROWS, D_MODEL), lambda l, n: (0, 0)),
            pl.BlockSpec((None, D_MODEL, tn), lambda l, n: (l, 0, n)),
            pl.BlockSpec((None, 1, tn), lambda l, n: (l, 0, n)),
        ],
        out_specs=pl.BlockSpec((None, COND_ROWS, tn), lambda l, n: (l, 0, n)),
        compiler_params=pltpu.CompilerParams(vmem_limit_bytes=VMEM_LIMIT),
        name="ada_mod",
    )(cond, ada_w, ada_b.reshape(depth, 1, cols))


def _head_rms(t, bd, w):
    hi, lo = _split_bf16(t * t)
    ss = _dot(hi, bd) + _dot(lo, bd)
    return t * lax.rsqrt(ss * (1.0 / HEAD_DIM) + EPS) * w


def _rope(t, cos, sin):
    width = t.shape[1]
    lane = lax.broadcasted_iota(jnp.int32, t.shape, 1)
    first = (lane % 32) < 16
    partner = jnp.where(first, pltpu.roll(t, width - 16, 1), pltpu.roll(t, 16, 1))
    return t * cos + partner * sin


def _spread_kv(t):
    lane = lax.broadcasted_iota(jnp.int32, t.shape, 1)
    low0 = jnp.where(lane < HEAD_DIM, t, 0.0)
    high1 = jnp.where(lane >= HEAD_DIM, t, 0.0)
    parts = [low0, pltpu.roll(low0, HEAD_DIM, 1), pltpu.roll(high1, HEAD_DIM, 1), high1]
    return jnp.concatenate(parts, axis=1).astype(BF16)


def _stream_specs(x, t_ctx):
    ctx_tiles = t_ctx // SEQ_TILE
    xc, xl = x if isinstance(x, tuple) else (x, x)
    lat0 = 0 if isinstance(x, tuple) else ctx_tiles
    specs = [pl.BlockSpec((SEQ_TILE, D_MODEL), lambda i: (jnp.minimum(i, ctx_tiles - 1), 0)),
             pl.BlockSpec((SEQ_TILE, D_MODEL), lambda i: (lat0 + jnp.maximum(i - ctx_tiles, 0), 0))]
    return specs, (xc, xl)


def _stream_tile(xc_ref, xl_ref, ctx_tiles):
    return jnp.where(pl.program_id(0) < ctx_tiles, xc_ref[...], xl_ref[...])


def _in_kernel(xc_ref, xl_ref, mod_ref, nw_ref, w_ref, qw_ref, kw_ref, bd_ref, cos_ref, sin_ref,
               q_ref, kx_ref, vx_ref, k_ref, v_ref, u_ref, *, ctx_tiles):
    x = _stream_tile(xc_ref, xl_ref, ctx_tiles)
    h = _rms_mod(x, nw_ref[...], mod_ref[1:2, :], mod_ref[0:1, :])
    proj = _dot(h.astype(BF16), w_ref[...])
    q = proj[:, :ATTN_WIDTH]
    k = proj[:, ATTN_WIDTH:ATTN_WIDTH + KV_WIDTH]
    v = proj[:, ATTN_WIDTH + KV_WIDTH:ATTN_WIDTH + 2 * KV_WIDTH]
    u_ref[...] = proj[:, ATTN_WIDTH + 2 * KV_WIDTH:]
    cos = cos_ref[...]
    sin = sin_ref[...]
    q = _head_rms(q, bd_ref[...], qw_ref[...])
    k = _head_rms(k, bd_ref[:KV_WIDTH, :KV_WIDTH], kw_ref[...])
    q_ref[...] = (_rope(q, cos, sin) * (HEAD_DIM ** -0.5)).astype(BF16)
    k = _rope(k, cos[:, :KV_WIDTH], sin[:, :KV_WIDTH])

    @pl.when(pl.program_id(0) < ctx_tiles)
    def _():
        k_ref[...] = k
        v_ref[...] = v

    kx_ref[...] = _spread_kv(k)
    vx_ref[...] = _spread_kv(v)


def _in_proj(l, x, t, mod, norm_w, w_in, qw, kw, bd, cos_t, sin_t, t_ctx, dec_seq):
    n_rope = dec_seq // SEQ_TILE
    ctx_tiles = t_ctx // SEQ_TILE

    def rope_idx(i):
        return (jnp.where(i * SEQ_TILE < t_ctx, 0, 1 + (i - t_ctx // SEQ_TILE) % n_rope), 0)

    row = lambda i: (i, 0)
    ctx_row = lambda i: (jnp.minimum(i, ctx_tiles - 1), 0)
    x_specs, x_ops = _stream_specs(x, t_ctx)
    return pl.pallas_call(
        functools.partial(_in_kernel, ctx_tiles=ctx_tiles),
        out_shape=(jax.ShapeDtypeStruct((t, ATTN_WIDTH), BF16),
                   jax.ShapeDtypeStruct((t, KVX_WIDTH), BF16),
                   jax.ShapeDtypeStruct((t, KVX_WIDTH), BF16),
                   jax.ShapeDtypeStruct((t_ctx, KV_WIDTH), F32),
                   jax.ShapeDtypeStruct((t_ctx, KV_WIDTH), F32),
                   jax.ShapeDtypeStruct((t, POOL_WIDTH), F32)),
        grid=(t // SEQ_TILE,),
        in_specs=x_specs + [
            pl.BlockSpec((None, None, N_MOD, D_MODEL),
                         lambda i: (l, _cond_of_tile(i, SEQ_TILE, t_ctx, dec_seq), 0, 0)),
            pl.BlockSpec((None, 1, D_MODEL), lambda i: (l, 0, 0)),
            pl.BlockSpec((None, D_MODEL, IN_COLS), lambda i: (l, 0, 0)),
            pl.BlockSpec((None, 1, ATTN_WIDTH), lambda i: (l, 0, 0)),
            pl.BlockSpec((None, 1, KV_WIDTH), lambda i: (l, 0, 0)),
            pl.BlockSpec((ATTN_WIDTH, ATTN_WIDTH), lambda i: (0, 0)),
            pl.BlockSpec((SEQ_TILE, ATTN_WIDTH), rope_idx),
            pl.BlockSpec((SEQ_TILE, ATTN_WIDTH), rope_idx),
        ],
        out_specs=(pl.BlockSpec((SEQ_TILE, ATTN_WIDTH), row),
                   pl.BlockSpec((SEQ_TILE, KVX_WIDTH), row),
                   pl.BlockSpec((SEQ_TILE, KVX_WIDTH), row),
                   pl.BlockSpec((SEQ_TILE, KV_WIDTH), ctx_row),
                   pl.BlockSpec((SEQ_TILE, KV_WIDTH), ctx_row),
                   pl.BlockSpec((SEQ_TILE, POOL_WIDTH), row)),
        compiler_params=pltpu.CompilerParams(dimension_semantics=("arbitrary",), vmem_limit_bytes=VMEM_LIMIT),
        name=f"in_proj_{l}",
    )(*x_ops, mod, norm_w, w_in, qw, kw, bd, cos_t, sin_t)


def _attend(q_ref, key_refs, val_refs, mix_ref):
    heads_per_kv = N_HEADS // N_KV_HEADS
    for pair in range(ATTN_WIDTH // LANES):
        kv = (2 * pair) // heads_per_kv
        qb = q_ref[:, pair * LANES:(pair + 1) * LANES]
        out = None
        for half in range(2):
            c0 = (2 * kv + half) * LANES
            scores = [_dot_nt(qb, k[:, c0:c0 + LANES]) for k in key_refs]
            m = functools.reduce(jnp.maximum, [jnp.max(s, axis=-1, keepdims=True) for s in scores])
            ps = [jnp.exp(s - m) for s in scores]
            denom = functools.reduce(jnp.add, [jnp.sum(p, axis=-1, keepdims=True) for p in ps])
            o = functools.reduce(jnp.add, [_dot(p.astype(BF16), v[:, c0:c0 + LANES])
                                           for p, v in zip(ps, val_refs)])
            o = o / denom
            out = o if out is None else out + o
        mix_ref[:, pair * LANES:(pair + 1) * LANES] = out.astype(BF16)


def _pool_tile(u_ref, row0, slab0, slab_rows, seq_len, pw_ref, ps_ref, mix_ref):
    rows = SEQ_TILE
    r = row0 + lax.broadcasted_iota(jnp.int32, (rows, slab_rows), 0)
    c = slab0 + lax.broadcasted_iota(jnp.int32, (rows, slab_rows), 1)
    offset = c - r
    pos = row0 + lax.broadcasted_iota(jnp.int32, (rows, POOL_GC), 0)
    for g in range(N_POOL_GROUPS):
        half = 1 << g
        lanes = slice(g * POOL_GC, (g + 1) * POOL_GC)
        band = jnp.where(offset >= -half, jnp.where(offset < half, 1.0, 0.0), 0.0).astype(BF16)
        hi, lo = _split_bf16(u_ref[pl.ds(slab0, slab_rows), lanes])
        window_sum = _dot(band, hi) + _dot(band, lo)
        count = jnp.minimum(pos + half, seq_len) - jnp.maximum(pos - half, 0)
        d = window_sum / count.astype(F32) - u_ref[pl.ds(row0, rows), lanes]
        y = _dot(d.astype(BF16), pw_ref[g]) * ps_ref[:, lanes]
        mix_ref[:, ATTN_WIDTH + g * POOL_GC:ATTN_WIDTH + (g + 1) * POOL_GC] = y.astype(BF16)


def _mixer_kernel(q_ref, kc_ref, vc_ref, kl_ref, vl_ref, ck_ref, cv_ref, uc_ref, ul_ref,
                  xc_ref, xl_ref, mod_ref, nw_ref, wo_ref, pw_ref, ps_ref, x1_ref, h2_ref, mix_ref,
                  *, ctx_tiles, lat_tiles):
    i = pl.program_id(0)

    @pl.when(i < ctx_tiles)
    def _():
        _attend(q_ref, [kc_ref], [vc_ref], mix_ref)
        _pool_tile(uc_ref, 0, 0, SEQ_TILE, SEQ_TILE, pw_ref, ps_ref, mix_ref)

    @pl.when(i >= ctx_tiles)
    def _():
        _attend(q_ref, [kl_ref, ck_ref], [vl_ref, cv_ref], mix_ref)
        seq_len = lat_tiles * SEQ_TILE
        row0 = pl.multiple_of(((i - ctx_tiles) % lat_tiles) * SEQ_TILE, SEQ_TILE)
        slab_rows = 2 * SEQ_TILE
        slab0 = pl.multiple_of(jnp.clip(row0 - SEQ_TILE // 2, 0, seq_len - slab_rows), SEQ_TILE // 2)
        _pool_tile(ul_ref, row0, slab0, slab_rows, seq_len, pw_ref, ps_ref, mix_ref)

    x1 = _stream_tile(xc_ref, xl_ref, ctx_tiles) + mod_ref[2:3, :] * _dot(mix_ref[...], wo_ref[...])
    x1_ref[...] = x1
    h2_ref[...] = _rms_mod(x1, nw_ref[...], mod_ref[4:5, :], mod_ref[3:4, :]).astype(h2_ref.dtype)


def _mixer(l, q, kx, vx, ckx, cvx, u, x, mod, norm_w, w_o, pool_w, pool_scale, h2_dtype, t_ctx, dec_seq):
    t = q.shape[0]
    past = ckx.shape[2]
    ctx_tiles = t_ctx // SEQ_TILE
    lat_tiles = dec_seq // SEQ_TILE
    lat_seq0 = t_ctx // dec_seq
    x_specs, x_ops = _stream_specs(x, t_ctx)
    row = lambda i: (i, 0)
    ctx_row = lambda i: (jnp.minimum(i, ctx_tiles - 1), 0)
    lat_seq = lambda i: (lat_seq0 + jnp.maximum(i - ctx_tiles, 0) // lat_tiles, 0)
    cache = lambda i: (jnp.maximum(i - ctx_tiles, 0) // lat_tiles, l, 0, 0)
    return pl.pallas_call(
        functools.partial(_mixer_kernel, ctx_tiles=ctx_tiles, lat_tiles=lat_tiles),
        out_shape=(jax.ShapeDtypeStruct((t, D_MODEL), F32),
                   jax.ShapeDtypeStruct((t, D_MODEL), h2_dtype)),
        grid=(t // SEQ_TILE,),
        in_specs=[
            pl.BlockSpec((SEQ_TILE, ATTN_WIDTH), row),
            pl.BlockSpec((SEQ_TILE, KVX_WIDTH), ctx_row),
            pl.BlockSpec((SEQ_TILE, KVX_WIDTH), ctx_row),
            pl.BlockSpec((dec_seq, KVX_WIDTH), lat_seq),
            pl.BlockSpec((dec_seq, KVX_WIDTH), lat_seq),
            pl.BlockSpec((None, None, past, KVX_WIDTH), cache),
            pl.BlockSpec((None, None, past, KVX_WIDTH), cache),
            pl.BlockSpec((SEQ_TILE, POOL_WIDTH), ctx_row),
            pl.BlockSpec((dec_seq, POOL_WIDTH), lat_seq),
            *x_specs,
            pl.BlockSpec((None, None, N_MOD, D_MODEL),
                         lambda i: (l, _cond_of_tile(i, SEQ_TILE, t_ctx, dec_seq), 0, 0)),
            pl.BlockSpec((None, 1, D_MODEL), lambda i: (l, 0, 0)),
            pl.BlockSpec((None, D_MODEL, D_MODEL), lambda i: (l, 0, 0)),
            pl.BlockSpec((None, N_POOL_GROUPS, POOL_GC, POOL_GC), lambda i: (l, 0, 0, 0)),
            pl.BlockSpec((None, 1, POOL_WIDTH), lambda i: (l, 0, 0)),
        ],
        out_specs=(pl.BlockSpec((SEQ_TILE, D_MODEL), row),
                   pl.BlockSpec((SEQ_TILE, D_MODEL), row)),
        scratch_shapes=[pltpu.VMEM((SEQ_TILE, D_MODEL), BF16)],
        compiler_params=pltpu.CompilerParams(vmem_limit_bytes=VMEM_LIMIT),
        name=f"mixer_{l}",
    )(q, kx, vx, kx, vx, ckx, cvx, u, u, *x_ops, mod, norm_w, w_o, pool_w, pool_scale)


def _swiglu_step(h, w1_ref, w3_ref, w2_ref, acc_ref):
    a = _dot(h, w1_ref[...])
    b = _dot(h, w3_ref[...])
    gated = (a * jax.nn.sigmoid(a) * b).astype(BF16)
    acc_ref[...] += _dot(gated, w2_ref[...])


def _ffn_kernel(h_ref, w1_ref, w3_ref, w2_ref, x1_ref, mod_ref, o_ref, acc_ref):
    f = pl.program_id(1)

    @pl.when(f == 0)
    def _():
        acc_ref[...] = jnp.zeros_like(acc_ref)

    _swiglu_step(h_ref[...], w1_ref, w3_ref, w2_ref, acc_ref)

    @pl.when(f == pl.num_programs(1) - 1)
    def _():
        o_ref[...] = x1_ref[...] + mod_ref[5:6, :] * acc_ref[...]


def _dense_ffn(l, h2, x1, mod, w1, w3, w2, t_ctx, dec_seq):
    t = x1.shape[0]
    row = lambda i, f: (i, 0)
    return pl.pallas_call(
        _ffn_kernel,
        out_shape=jax.ShapeDtypeStruct((t, D_MODEL), F32),
        grid=(t // FFN_TILE, D_FF // FF_CHUNK),
        in_specs=[
            pl.BlockSpec((FFN_TILE, D_MODEL), row),
            pl.BlockSpec((None, D_MODEL, FF_CHUNK), lambda i, f: (l // 2, 0, f)),
            pl.BlockSpec((None, D_MODEL, FF_CHUNK), lambda i, f: (l // 2, 0, f)),
            pl.BlockSpec((None, FF_CHUNK, D_MODEL), lambda i, f: (l // 2, f, 0)),
            pl.BlockSpec((FFN_TILE, D_MODEL), row),
            pl.BlockSpec((None, None, N_MOD, D_MODEL),
                         lambda i, f: (l, _cond_of_tile(i, FFN_TILE, t_ctx, dec_seq), 0, 0)),
        ],
        out_specs=pl.BlockSpec((FFN_TILE, D_MODEL), row),
        scratch_shapes=[pltpu.VMEM((FFN_TILE, D_MODEL), F32)],
        compiler_params=pltpu.CompilerParams(vmem_limit_bytes=VMEM_LIMIT),
        name=f"dense_ffn_{l}",
    )(h2, w1, w3, w2, x1, mod)


def _route_kernel(h_ref, rw_ref, rb_ref, meta_ref, wts_ref, cnt_ref, carry_ref):
    @pl.when(pl.program_id(0) == 0)
    def _():
        carry_ref[...] = jnp.zeros_like(carry_ref)

    h = h_ref[...]
    hi, lo = _split_bf16(h)
    rw = rw_ref[...]
    whi, wlo = _split_bf16(rw)
    logits = _dot(hi, whi) + _dot(lo, whi) + _dot(hi, wlo) + rb_ref[...]
    lane = lax.broadcasted_iota(jnp.int32, logits.shape, 1)
    lane_f = lane.astype(F32)
    m0 = jnp.max(logits, axis=-1, keepdims=True)
    e0 = jnp.min(jnp.where(logits == m0, lane_f, float(LANES)), axis=-1, keepdims=True)
    rest = jnp.where(lane_f == e0, -jnp.inf, logits)
    m1 = jnp.max(rest, axis=-1, keepdims=True)
    e1 = jnp.min(jnp.where(rest == m1, lane_f, float(LANES)), axis=-1, keepdims=True)
    ex = jnp.exp(m1 - m0)
    w0 = 1.0 / (1.0 + ex)
    w1 = ex / (1.0 + ex)

    n = logits.shape[0]
    sel0 = lane_f == e0
    sel1 = lane_f == e1
    mask = jnp.where(sel0 | sel1, 1.0, 0.0)
    r = lax.broadcasted_iota(jnp.int32, (n, n), 0)
    c = lax.broadcasted_iota(jnp.int32, (n, n), 1)
    below = jnp.where(c < r, 1.0, 0.0).astype(BF16)
    rank = _dot(below, mask.astype(BF16)) + carry_ref[0:1, :]
    r0 = jnp.sum(jnp.where(sel0, rank, 0.0), axis=-1, keepdims=True)
    r1 = jnp.sum(jnp.where(sel1, rank, 0.0), axis=-1, keepdims=True)
    carry_ref[...] = carry_ref[...] + jnp.sum(mask, axis=0, keepdims=True)
    cnt_ref[...] = carry_ref[...]

    cols = jnp.where(lane == 0, e0, jnp.where(lane == 1, e1, jnp.where(lane == 2, r0, jnp.where(lane == 3, r1, 0.0))))
    meta_ref[...] = cols.T[:META_ROWS, :].astype(jnp.int32)
    wts_ref[...] = jnp.where(lane == 0, w0, jnp.where(lane == 1, w1, 0.0))


def _route(h2, router_w, router_b):
    t = h2.shape[0]
    rw = jnp.zeros((D_MODEL, LANES), F32).at[:, :N_EXPERTS].set(router_w)
    rb = jnp.full((1, LANES), NEG_BIG, F32).at[0, :N_EXPERTS].set(router_b)
    row = lambda i: (i, 0)
    return pl.pallas_call(
        _route_kernel,
        out_shape=(jax.ShapeDtypeStruct((META_ROWS, t), jnp.int32),
                   jax.ShapeDtypeStruct((t, LANES), F32),
                   jax.ShapeDtypeStruct((8, LANES), F32)),
        grid=(t // ROUTE_TILE,),
        in_specs=[
            pl.BlockSpec((ROUTE_TILE, D_MODEL), row),
            pl.BlockSpec((D_MODEL, LANES), lambda i: (0, 0)),
            pl.BlockSpec((1, LANES), lambda i: (0, 0)),
        ],
        out_specs=(pl.BlockSpec((META_ROWS, ROUTE_TILE), lambda i: (0, i)),
                   pl.BlockSpec((ROUTE_TILE, LANES), row),
                   pl.BlockSpec((8, LANES), lambda i: (0, 0))),
        scratch_shapes=[pltpu.VMEM((8, LANES), F32)],
        compiler_params=pltpu.CompilerParams(dimension_semantics=("arbitrary",),
                                             vmem_limit_bytes=VMEM_LIMIT),
        name="route",
    )(h2, rw, rb)


def _row_copy(src_ref, src_row, dst_ref, dst_row, sem):
    return pltpu.make_async_copy(src_ref.at[pl.ds(src_row, 1)], dst_ref.at[pl.ds(dst_row, 1)], sem)


def _dispatch_kernel(p0_ref, p1_ref, pad_ref, h_ref, xs_ref, zero_ref, sem):
    i = pl.program_id(0)
    base = i * MOVE_TILE

    @pl.when(i == 0)
    def _():
        zero_ref[...] = jnp.zeros_like(zero_ref)
        fills = [(pad_ref[N_EXPERTS + e] == 1,
                  pltpu.make_async_copy(zero_ref, xs_ref.at[pl.ds(pl.multiple_of(pad_ref[e], FFN_TILE), FFN_TILE)], sem))
                 for e in range(N_EXPERTS)]
        n_tiles = xs_ref.shape[0] // FFN_TILE
        fills += [(tile >= pad_ref[2 * N_EXPERTS],
                   pltpu.make_async_copy(zero_ref, xs_ref.at[pl.ds(tile * FFN_TILE, FFN_TILE)], sem))
                  for tile in range(n_tiles - N_EXPERTS, n_tiles)]
        for step in ("start", "wait"):
            for wanted, fill in fills:
                @pl.when(wanted)
                def _(fill=fill, step=step):
                    getattr(fill, step)()

    def start_rows(b, carry):
        for u in range(ROW_UNROLL):
            r = b * ROW_UNROLL + u
            _row_copy(h_ref, r, xs_ref, p0_ref[base + r], sem).start()
            _row_copy(h_ref, r, xs_ref, p1_ref[base + r], sem).start()
        return carry

    def wait_rows(b, carry):
        for u in range(ROW_UNROLL):
            r = b * ROW_UNROLL + u
            _row_copy(h_ref, r, xs_ref, p0_ref[base + r], sem).wait()
            _row_copy(h_ref, r, xs_ref, p1_ref[base + r], sem).wait()
        return carry

    lax.fori_loop(0, MOVE_TILE // ROW_UNROLL, start_rows, 0)
    lax.fori_loop(0, MOVE_TILE // ROW_UNROLL, wait_rows, 0)


def _dispatch(h2, pos0, pos1, pad, n_slots):
    t = h2.shape[0]
    return pl.pallas_call(
        _dispatch_kernel,
        out_shape=jax.ShapeDtypeStruct((n_slots, D_MODEL), F32),
        grid_spec=pltpu.PrefetchScalarGridSpec(
            num_scalar_prefetch=3,
            grid=(t // MOVE_TILE,),
            in_specs=[pl.BlockSpec((MOVE_TILE, D_MODEL), lambda i, p0, p1, pad: (i, 0))],
            out_specs=pl.BlockSpec(memory_space=pl.ANY),
            scratch_shapes=[pltpu.VMEM((FFN_TILE, D_MODEL), F32), pltpu.SemaphoreType.DMA],
        ),
        compiler_params=pltpu.CompilerParams(dimension_semantics=("arbitrary",),
                                             vmem_limit_bytes=VMEM_LIMIT),
        name="dispatch",
    )(pos0, pos1, pad, h2)


def _expert_kernel(te_ref, tv_ref, x_ref, w1_ref, w3_ref, w2_ref, o_ref, acc_ref):
    i = pl.program_id(0)
    f = pl.program_id(1)
    last = pl.num_programs(1) - 1

    @pl.when(f == 0)
    def _():
        acc_ref[...] = jnp.zeros_like(acc_ref)

    @pl.when(tv_ref[i] == 1)
    def _():
        _swiglu_step(x_ref[...].astype(BF16), w1_ref, w3_ref, w2_ref, acc_ref)

    @pl.when(f == last)
    def _():
        o_ref[...] = acc_ref[...]


def _expert_ffn(xs, tile_expert, tile_valid, w1, w3, w2):
    n_slots = xs.shape[0]
    row = lambda i, f, te, tv: (i, 0)
    return pl.pallas_call(
        _expert_kernel,
        out_shape=jax.ShapeDtypeStruct((n_slots, D_MODEL), F32),
        grid_spec=pltpu.PrefetchScalarGridSpec(
            num_scalar_prefetch=2,
            grid=(n_slots // FFN_TILE, D_FF // FF_CHUNK),
            in_specs=[
                pl.BlockSpec((FFN_TILE, D_MODEL), row),
                pl.BlockSpec((None, D_MODEL, FF_CHUNK), lambda i, f, te, tv: (te[i], 0, f)),
                pl.BlockSpec((None, D_MODEL, FF_CHUNK), lambda i, f, te, tv: (te[i], 0, f)),
                pl.BlockSpec((None, FF_CHUNK, D_MODEL), lambda i, f, te, tv: (te[i], f, 0)),
            ],
            out_specs=pl.BlockSpec((FFN_TILE, D_MODEL), row),
            scratch_shapes=[pltpu.VMEM((FFN_TILE, D_MODEL), F32)],
        ),
        compiler_params=pltpu.CompilerParams(vmem_limit_bytes=VMEM_LIMIT),
        name="expert_ffn",
    )(tile_expert, tile_valid, xs, w1, w3, w2)


def _combine_kernel(p0_ref, p1_ref, ys_ref, x1_ref, mod_ref, wts_ref, oc_ref, ol_ref, y0_ref, y1_ref, sem,
                    *, ctx_tiles):
    i = pl.program_id(0)
    slot = i % 2

    def rows(tile, slot, step):
        base = tile * MOVE_TILE

        def body(b, carry):
            for u in range(ROW_UNROLL):
                r = b * ROW_UNROLL + u
                getattr(_row_copy(ys_ref, p0_ref[base + r], y0_ref.at[slot], r, sem.at[slot]), step)()
                getattr(_row_copy(ys_ref, p1_ref[base + r], y1_ref.at[slot], r, sem.at[slot]), step)()
            return carry

        lax.fori_loop(0, MOVE_TILE // ROW_UNROLL, body, 0)

    @pl.when(i == 0)
    def _():
        rows(0, 0, "start")

    @pl.when(i + 1 < pl.num_programs(0))
    def _():
        rows(i + 1, 1 - slot, "start")

    rows(i, slot, "wait")
    wts = wts_ref[...]
    moe = wts[:, 0:1] * y0_ref[slot] + wts[:, 1:2] * y1_ref[slot]
    out = x1_ref[...] + mod_ref[5:6, :] * moe

    @pl.when(i < ctx_tiles)
    def _():
        oc_ref[...] = out

    @pl.when(i >= ctx_tiles)
    def _():
        ol_ref[...] = out


def _combine(l, ys, pos0, pos1, x1, mod, wts, t_ctx, dec_seq):
    t = x1.shape[0]
    ctx_tiles = t_ctx // MOVE_TILE
    return pl.pallas_call(
        functools.partial(_combine_kernel, ctx_tiles=ctx_tiles),
        out_shape=(jax.ShapeDtypeStruct((t_ctx, D_MODEL), F32),
                   jax.ShapeDtypeStruct((t - t_ctx, D_MODEL), F32)),
        grid_spec=pltpu.PrefetchScalarGridSpec(
            num_scalar_prefetch=2,
            grid=(t // MOVE_TILE,),
            in_specs=[
                pl.BlockSpec(memory_space=pl.ANY),
                pl.BlockSpec((MOVE_TILE, D_MODEL), lambda i, p0, p1: (i, 0)),
                pl.BlockSpec((None, None, N_MOD, D_MODEL),
                             lambda i, p0, p1: (l, _cond_of_tile(i, MOVE_TILE, t_ctx, dec_seq), 0, 0)),
                pl.BlockSpec((MOVE_TILE, LANES), lambda i, p0, p1: (i, 0)),
            ],
            out_specs=(pl.BlockSpec((MOVE_TILE, D_MODEL), lambda i, p0, p1: (jnp.minimum(i, ctx_tiles - 1), 0)),
                       pl.BlockSpec((MOVE_TILE, D_MODEL), lambda i, p0, p1: (jnp.maximum(i - ctx_tiles, 0), 0))),
            scratch_shapes=[pltpu.VMEM((2, MOVE_TILE, D_MODEL), F32),
                            pltpu.VMEM((2, MOVE_TILE, D_MODEL), F32),
                            pltpu.SemaphoreType.DMA((2,))],
        ),
        compiler_params=pltpu.CompilerParams(dimension_semantics=("arbitrary",),
                                             vmem_limit_bytes=VMEM_LIMIT),
        name="combine",
    )(pos0, pos1, ys, x1, mod, wts)


def _moe(l, h2, x1, mod, router_w, router_b, w1, w3, w2, t_ctx, dec_seq):
    t = h2.shape[0]
    meta, wts, counts = _route(h2, router_w, router_b)
    counts = counts[0, :N_EXPERTS].astype(jnp.int32)
    tiles_per_expert = (counts + FFN_TILE - 1) // FFN_TILE
    tile_end = jnp.cumsum(tiles_per_expert)
    slot_start = (tile_end - tiles_per_expert) * FFN_TILE
    n_tiles = (2 * t) // FFN_TILE + N_EXPERTS
    tile_id = jnp.arange(n_tiles, dtype=jnp.int32)
    tile_expert = jnp.minimum(jnp.sum(tile_id[:, None] >= tile_end[None, :], axis=1), N_EXPERTS - 1)
    tile_valid = (tile_id < tile_end[-1]).astype(jnp.int32)

    def slot_of(expert, rank):
        start = sum(jnp.where(expert == e, slot_start[e], 0) for e in range(N_EXPERTS))
        return start + rank

    pos0 = slot_of(meta[0], meta[2])
    pos1 = slot_of(meta[1], meta[3])
    pad = jnp.concatenate([jnp.maximum(tile_end - 1, 0) * FFN_TILE, (tiles_per_expert > 0).astype(jnp.int32),
                           tile_end[-1:]])
    xs = _dispatch(h2, pos0, pos1, pad.astype(jnp.int32), n_tiles * FFN_TILE)
    ys = _expert_ffn(xs, tile_expert.astype(jnp.int32), tile_valid, w1, w3, w2)
    return _combine(l, ys, pos0, pos1, x1, mod, wts, t_ctx, dec_seq)


def _rope_tables(dec_seq):
    pos = jnp.arange(dec_seq)
    row = (pos // GRID_W).astype(F32)
    col = (pos % GRID_W).astype(F32)
    nf = HEAD_DIM // 4
    inv = ROPE_THETA ** (-jnp.arange(nf, dtype=F32) / nf)
    ang_r = row[:, None] * inv
    ang_c = col[:, None] * inv
    cos = jnp.concatenate([jnp.cos(ang_r)] * 2 + [jnp.cos(ang_c)] * 2, axis=1)
    sin = jnp.concatenate([-jnp.sin(ang_r), jnp.sin(ang_r), -jnp.sin(ang_c), jnp.sin(ang_c)], axis=1)
    cos = jnp.concatenate([jnp.ones((SEQ_TILE, HEAD_DIM), F32), cos], axis=0)
    sin = jnp.concatenate([jnp.zeros((SEQ_TILE, HEAD_DIM), F32), sin], axis=0)
    return jnp.tile(cos, (1, N_HEADS)), jnp.tile(sin, (1, N_HEADS))


def _spread_cache(cache):
    h0, h1 = cache[..., 0, :], cache[..., 1, :]
    z = jnp.zeros_like(h0)
    return jnp.concatenate([h0, z, z, h0, h1, z, z, h1], axis=-1).astype(BF16)


def kernel(x_prompt, x_sample, cache_k, cache_v, c, c_ctx, norm1_w, norm2_w, ada_w, ada_b, w_in, w_o,
           q_norm_w, k_norm_w, pool_w, pool_scale, ffn_w1, ffn_w3, ffn_w2, router_w, router_b,
           moe_w1, moe_w3, moe_w2):
    n_ctx, seq, _ = x_prompt.shape
    n_lat, dec_seq, _ = x_sample.shape
    depth = w_in.shape[0]
    past = cache_k.shape[2]
    t_ctx = n_ctx * seq
    assert seq == SEQ_TILE and dec_seq % SEQ_TILE == 0 and t_ctx % dec_seq == 0
    assert t_ctx % FFN_TILE == 0 and dec_seq % FFN_TILE == 0 and 1 + n_lat <= COND_ROWS

    t = t_ctx + n_lat * dec_seq
    x = (x_prompt.reshape(t_ctx, D_MODEL), x_sample.reshape(n_lat * dec_seq, D_MODEL))
    cond = jnp.zeros((COND_ROWS, D_MODEL), F32).at[0].set(c_ctx).at[1:1 + n_lat].set(c)
    mod = _ada(cond, ada_w, ada_b).reshape(depth, COND_ROWS, N_MOD, D_MODEL)

    cos_t, sin_t = _rope_tables(dec_seq)
    head = jnp.arange(ATTN_WIDTH) // HEAD_DIM
    bd = (head[:, None] == head[None, :]).astype(BF16)
    qw = jnp.tile(q_norm_w, (1, N_HEADS)).reshape(depth, 1, ATTN_WIDTH)
    kw = jnp.tile(k_norm_w, (1, N_KV_HEADS)).reshape(depth, 1, KV_WIDTH)
    n1 = norm1_w.reshape(depth, 1, D_MODEL)
    n2 = norm2_w.reshape(depth, 1, D_MODEL)
    ckx = _spread_cache(cache_k)
    cvx = _spread_cache(cache_v)
    w_in_b = w_in.astype(BF16)
    w_o_b = w_o.astype(BF16)
    pool_w_b = pool_w.astype(BF16)
    pool_s = pool_scale.reshape(depth, 1, POOL_WIDTH)

    new_k, new_v = [], []
    for l in range(depth):
        q, kx, vx, k, v, u = _in_proj(l, x, t, mod, n1, w_in_b, qw, kw, bd, cos_t, sin_t, t_ctx, dec_seq)
        new_k.append(k.reshape(n_ctx, seq, N_KV_HEADS, HEAD_DIM))
        new_v.append(v.reshape(n_ctx, seq, N_KV_HEADS, HEAD_DIM))
        h2_dtype = BF16 if l % 2 == 0 else F32
        x1, h2 = _mixer(l, q, kx, vx, ckx, cvx, u, x, mod, n2, w_o_b, pool_w_b, pool_s, h2_dtype, t_ctx, dec_seq)
        if l % 2 == 0:
            x = _dense_ffn(l, h2, x1, mod, ffn_w1.astype(BF16), ffn_w3.astype(BF16), ffn_w2.astype(BF16),
                           t_ctx, dec_seq)
        else:
            x = _moe(l, h2, x1, mod, router_w[l // 2], router_b[l // 2],
                     moe_w1[l // 2].astype(BF16), moe_w3[l // 2].astype(BF16), moe_w2[l // 2].astype(BF16),
                     t_ctx, dec_seq)

    y_ctx, y_lat = x if isinstance(x, tuple) else (x[:t_ctx], x[t_ctx:])
    y_prompt = y_ctx.reshape(n_ctx, seq, D_MODEL)
    y_sample = y_lat.reshape(n_lat, dec_seq, D_MODEL)
    return y_prompt, y_sample, jnp.stack(new_k, axis=1), jnp.stack(new_v, axis=1)
```

```python
import functools

import jax
import jax.numpy as jnp
import numpy as np
from jax import lax
from jax.experimental import pallas as pl
from jax.experimental.pallas import tpu as pltpu

F32 = jnp.float32
BF16 = jnp.bfloat16

D_MODEL = 1024
HEAD_DIM = 64
N_HEADS = 8
N_KV_HEADS = 2
ATTN_WIDTH = N_HEADS * HEAD_DIM
KV_WIDTH = N_KV_HEADS * HEAD_DIM
KVX_WIDTH = 2 * N_KV_HEADS * 2 * HEAD_DIM
POOL_WIDTH = D_MODEL - ATTN_WIDTH
N_POOL_GROUPS = 4
POOL_GC = POOL_WIDTH // N_POOL_GROUPS
IN_COLS = ATTN_WIDTH + 2 * KV_WIDTH + POOL_WIDTH
D_FF = 2816
N_EXPERTS = 8
GRID_W = 64
ROPE_THETA = 10000.0
EPS = 1e-6
N_MOD = 6

LANES = 128
SEQ_TILE = 256
ROW_TILE = 2 * SEQ_TILE
FFN_TILE = 512
FF_SLAB = 256
N_SLABS = D_FF // FF_SLAB
ROUTE_TILE = 512
META_ROWS = 8
MOVE_TILE = 512
ROW_UNROLL = 8
COND_ROWS = 16
VMEM_LIMIT = 48 * 1024 * 1024
FFN_VMEM_LIMIT = 56 * 1024 * 1024
NEG_BIG = -1e30


def _cond_of_tile(i, tile, t_ctx, dec_seq):
    tok = i * tile
    return jnp.where(tok < t_ctx, 0, 1 + (tok - t_ctx) // dec_seq)


def _dot(a, b):
    return jnp.dot(a, b, preferred_element_type=F32)


def _dot_nt(a, b):
    return lax.dot_general(a, b, (((1,), (1,)), ((), ())), preferred_element_type=F32)


def _split_bf16(x):
    hi = x.astype(BF16)
    lo = (x - hi.astype(F32)).astype(BF16)
    return hi, lo


def _rms_mod(x, nw, scale, shift):
    ms = jnp.mean(x * x, axis=-1, keepdims=True)
    y = x * lax.rsqrt(ms + EPS) * nw
    return y * (1.0 + scale) + shift


def _ada_kernel(c_ref, w_ref, b_ref, o_ref):
    c = c_ref[...]
    s = c * jax.nn.sigmoid(c)
    hi, lo = _split_bf16(s)
    w = w_ref[...]
    whi, wlo = _split_bf16(w)
    o_ref[...] = _dot(hi, whi) + _dot(lo, whi) + _dot(hi, wlo) + b_ref[...]


def _ada(cond, ada_w, ada_b):
    depth = ada_w.shape[0]
    cols = ada_w.shape[2]
    tn = cols // 4
    return pl.pallas_call(
        _ada_kernel,
        out_shape=jax.ShapeDtypeStruct((depth, COND_ROWS, cols), F32),
        grid=(depth, cols // tn),
        in_specs=[
            pl.BlockSpec((COND_ROWS, D_MODEL), lambda l, n: (0, 0)),
            pl.BlockSpec((None, D_MODEL, tn), lambda l, n: (l, 0, n)),
            pl.BlockSpec((None, 1, tn), lambda l, n: (l, 0, n)),
        ],
        out_specs=pl.BlockSpec((None, COND_ROWS, tn), lambda l, n: (l, 0, n)),
        compiler_params=pltpu.CompilerParams(vmem_limit_bytes=VMEM_LIMIT),
        name="ada_mod",
    )(cond, ada_w, ada_b.reshape(depth, 1, cols))


def _head_rms(t, bd, w):
    ss = _dot((t * t).astype(BF16), bd)
    return t * lax.rsqrt(ss * (1.0 / HEAD_DIM) + EPS) * w


def _rope(t, cos, sin):
    width = t.shape[1]
    lane = lax.broadcasted_iota(jnp.int32, t.shape, 1)
    first = (lane % 32) < 16
    partner = jnp.where(first, pltpu.roll(t, width - 16, 1), pltpu.roll(t, 16, 1))
    return t * cos + partner * sin


def _spread_kv(t, unit):
    lane = lax.broadcasted_iota(jnp.int32, t.shape, 1)
    low0 = jnp.where(lane < HEAD_DIM, t, jnp.where(lane == HEAD_DIM, unit, 0.0))
    high1 = jnp.where(lane >= HEAD_DIM, t, jnp.where(lane == 0, unit, 0.0))
    parts = [low0, pltpu.roll(low0, HEAD_DIM, 1), pltpu.roll(high1, HEAD_DIM, 1), high1]
    return jnp.concatenate(parts, axis=1).astype(BF16)


def _stream_specs(x, t_ctx):
    ctx_tiles = t_ctx // ROW_TILE
    xc, xl = x if isinstance(x, tuple) else (x, x)
    lat0 = 0 if isinstance(x, tuple) else ctx_tiles
    specs = [pl.BlockSpec((ROW_TILE, D_MODEL), lambda i: (jnp.minimum(i, ctx_tiles - 1), 0)),
             pl.BlockSpec((ROW_TILE, D_MODEL), lambda i: (lat0 + jnp.maximum(i - ctx_tiles, 0), 0))]
    return specs, (xc, xl)


def _stream_tile(xc_ref, xl_ref, ctx_tiles):
    return jnp.where(pl.program_id(0) < ctx_tiles, xc_ref[...], xl_ref[...])


def _in_kernel(xc_ref, xl_ref, mod_ref, nw_ref, w_ref, qw_ref, kw_ref, bd_ref, cos_ref, sin_ref,
               q_ref, kx_ref, vx_ref, k_ref, v_ref, u_ref, *, ctx_tiles):
    x = _stream_tile(xc_ref, xl_ref, ctx_tiles)
    h = _rms_mod(x, nw_ref[...], mod_ref[1:2, :], mod_ref[0:1, :])
    proj = _dot(h.astype(BF16), w_ref[...])
    q = proj[:, :ATTN_WIDTH]
    k = proj[:, ATTN_WIDTH:ATTN_WIDTH + KV_WIDTH]
    v = proj[:, ATTN_WIDTH + KV_WIDTH:ATTN_WIDTH + 2 * KV_WIDTH]
    u_ref[...] = proj[:, ATTN_WIDTH + 2 * KV_WIDTH:]
    cos = cos_ref[...]
    sin = sin_ref[...]
    q = _head_rms(q, bd_ref[...], qw_ref[...])
    k = _head_rms(k, bd_ref[:KV_WIDTH, :KV_WIDTH], kw_ref[...])
    q_ref[...] = (_rope(q, cos, sin) * (HEAD_DIM ** -0.5)).astype(BF16)
    k = _rope(k, cos[:, :KV_WIDTH], sin[:, :KV_WIDTH])

    @pl.when(pl.program_id(0) < ctx_tiles)
    def _():
        k_ref[...] = k
        v_ref[...] = v

    kx_ref[...] = _spread_kv(k, 0.0)
    vx_ref[...] = _spread_kv(v, 1.0)


def _in_proj(l, x, t, mod, norm_w, w_in, qw, kw, bd, cos_t, sin_t, t_ctx, dec_seq):
    n_rope = dec_seq // ROW_TILE
    ctx_tiles = t_ctx // ROW_TILE

    def rope_idx(i):
        return (jnp.where(i < ctx_tiles, 0, 1 + (i - ctx_tiles) % n_rope), 0)

    row = lambda i: (i, 0)
    ctx_row = lambda i: (jnp.minimum(i, ctx_tiles - 1), 0)
    x_specs, x_ops = _stream_specs(x, t_ctx)
    return pl.pallas_call(
        functools.partial(_in_kernel, ctx_tiles=ctx_tiles),
        out_shape=(jax.ShapeDtypeStruct((t, ATTN_WIDTH), BF16),
                   jax.ShapeDtypeStruct((t, KVX_WIDTH), BF16),
                   jax.ShapeDtypeStruct((t, KVX_WIDTH), BF16),
                   jax.ShapeDtypeStruct((t_ctx, KV_WIDTH), F32),
                   jax.ShapeDtypeStruct((t_ctx, KV_WIDTH), F32),
                   jax.ShapeDtypeStruct((t, POOL_WIDTH), F32)),
        grid=(t // ROW_TILE,),
        in_specs=x_specs + [
            pl.BlockSpec((None, None, N_MOD, D_MODEL),
                         lambda i: (l, _cond_of_tile(i, ROW_TILE, t_ctx, dec_seq), 0, 0)),
            pl.BlockSpec((None, 1, D_MODEL), lambda i: (l, 0, 0)),
            pl.BlockSpec((None, D_MODEL, IN_COLS), lambda i: (l, 0, 0)),
            pl.BlockSpec((None, 1, ATTN_WIDTH), lambda i: (l, 0, 0)),
            pl.BlockSpec((None, 1, KV_WIDTH), lambda i: (l, 0, 0)),
            pl.BlockSpec((ATTN_WIDTH, ATTN_WIDTH), lambda i: (0, 0)),
            pl.BlockSpec((ROW_TILE, ATTN_WIDTH), rope_idx),
            pl.BlockSpec((ROW_TILE, ATTN_WIDTH), rope_idx),
        ],
        out_specs=(pl.BlockSpec((ROW_TILE, ATTN_WIDTH), row),
                   pl.BlockSpec((ROW_TILE, KVX_WIDTH), row),
                   pl.BlockSpec((ROW_TILE, KVX_WIDTH), row),
                   pl.BlockSpec((ROW_TILE, KV_WIDTH), ctx_row),
                   pl.BlockSpec((ROW_TILE, KV_WIDTH), ctx_row),
                   pl.BlockSpec((ROW_TILE, POOL_WIDTH), row)),
        compiler_params=pltpu.CompilerParams(dimension_semantics=("arbitrary",), vmem_limit_bytes=VMEM_LIMIT),
        name=f"in_proj_{l}",
    )(*x_ops, mod, norm_w, w_in, qw, kw, bd, cos_t, sin_t)


def _attend(q_ref, rows, slabs, mix_ref):
    heads_per_kv = N_HEADS // N_KV_HEADS
    lane = lax.broadcasted_iota(jnp.int32, (rows.stop - rows.start, LANES), 1)
    for pair in range(ATTN_WIDTH // LANES):
        kv = (2 * pair) // heads_per_kv
        qb = q_ref[rows, pair * LANES:(pair + 1) * LANES]
        normed = []
        for half in range(2):
            c0 = (2 * kv + half) * LANES
            scores = [_dot_nt(qb, k[:, c0:c0 + LANES]) for k, _ in slabs]
            m = functools.reduce(jnp.maximum, [jnp.max(s, axis=-1, keepdims=True) for s in scores])
            ps = [jnp.exp((s - m).astype(BF16)) for s in scores]
            o = functools.reduce(jnp.add, [_dot(p, v[:, c0:c0 + LANES]) for p, (_, v) in zip(ps, slabs)])
            ones_lane = HEAD_DIM if half == 0 else 0
            normed.append(o / o[:, ones_lane:ones_lane + 1])
        out = jnp.where(lane < HEAD_DIM, normed[0], normed[1])
        mix_ref[rows, pair * LANES:(pair + 1) * LANES] = out.astype(BF16)


def _pool_rows(u_ref, seq0, row0, n_rows, slab0, slab_rows, seq_len, pw_ref, ps_ref, mix_ref, out_rows):
    r = row0 + lax.broadcasted_iota(jnp.int32, (n_rows, slab_rows), 0)
    c = slab0 + lax.broadcasted_iota(jnp.int32, (n_rows, slab_rows), 1)
    offset = c - r
    pos = row0 + lax.broadcasted_iota(jnp.int32, (n_rows, POOL_GC), 0)
    for g in range(N_POOL_GROUPS):
        half = 1 << g
        lanes = slice(g * POOL_GC, (g + 1) * POOL_GC)
        band = jnp.where(offset >= -half, jnp.where(offset < half, 1.0, 0.0), 0.0).astype(BF16)
        hi, lo = _split_bf16(u_ref[pl.ds(slab0 if seq0 == 0 else seq0 + slab0, slab_rows), lanes])
        window_sum = _dot(band, hi) + _dot(band, lo)
        count = jnp.minimum(pos + half, seq_len) - jnp.maximum(pos - half, 0)
        d = window_sum / count.astype(F32) - u_ref[pl.ds(row0 if seq0 == 0 else seq0 + row0, n_rows), lanes]
        y = _dot(d.astype(BF16), pw_ref[g]) * ps_ref[:, lanes]
        mix_ref[out_rows, ATTN_WIDTH + g * POOL_GC:ATTN_WIDTH + (g + 1) * POOL_GC] = y.astype(BF16)


def _mixer_kernel(q_ref, kc_ref, vc_ref, kl_ref, vl_ref, ck_ref, cv_ref, uc_ref, ul_ref,
                  xc_ref, xl_ref, mod_ref, nw_ref, wo_ref, pw_ref, ps_ref, x1_ref, h2_ref, mix_ref,
                  *, ctx_tiles, lat_tiles):
    i = pl.program_id(0)

    @pl.when(i < ctx_tiles)
    def _():
        for j in range(ROW_TILE // SEQ_TILE):
            rows = slice(j * SEQ_TILE, (j + 1) * SEQ_TILE)
            _attend(q_ref, rows, [(kc_ref.at[rows], vc_ref.at[rows])], mix_ref)
            _pool_rows(uc_ref, j * SEQ_TILE, 0, SEQ_TILE, 0, SEQ_TILE, SEQ_TILE, pw_ref, ps_ref, mix_ref, rows)

    @pl.when(i >= ctx_tiles)
    def _():
        rows = slice(0, ROW_TILE)
        cache = (_spread_kv(ck_ref[...], 0.0), _spread_kv(cv_ref[...], 1.0))
        _attend(q_ref, rows, [(kl_ref, vl_ref), cache], mix_ref)
        seq_len = lat_tiles * ROW_TILE
        row0 = pl.multiple_of(((i - ctx_tiles) % lat_tiles) * ROW_TILE, ROW_TILE)
        slab_rows = ROW_TILE + 2 * LANES
        slab0 = pl.multiple_of(jnp.clip(row0 - LANES, 0, seq_len - slab_rows), LANES)
        _pool_rows(ul_ref, 0, row0, ROW_TILE, slab0, slab_rows, seq_len, pw_ref, ps_ref, mix_ref, rows)

    x1 = _stream_tile(xc_ref, xl_ref, ctx_tiles) + mod_ref[2:3, :] * _dot(mix_ref[...], wo_ref[...])
    x1_ref[...] = x1
    h2_ref[...] = _rms_mod(x1, nw_ref[...], mod_ref[4:5, :], mod_ref[3:4, :]).astype(h2_ref.dtype)


def _mixer(l, q, kx, vx, cache_k, cache_v, u, x, mod, norm_w, w_o, pool_w, pool_scale, h2_dtype, t_ctx, dec_seq):
    t = q.shape[0]
    past = cache_k.shape[2]
    ctx_tiles = t_ctx // ROW_TILE
    lat_tiles = dec_seq // ROW_TILE
    lat_seq0 = t_ctx // dec_seq
    x_specs, x_ops = _stream_specs(x, t_ctx)
    row = lambda i: (i, 0)
    ctx_row = lambda i: (jnp.minimum(i, ctx_tiles - 1), 0)
    lat_seq = lambda i: (lat_seq0 + jnp.maximum(i - ctx_tiles, 0) // lat_tiles, 0)
    cache = lambda i: (jnp.maximum(i - ctx_tiles, 0) // lat_tiles, l, 0, 0)
    return pl.pallas_call(
        functools.partial(_mixer_kernel, ctx_tiles=ctx_tiles, lat_tiles=lat_tiles),
        out_shape=(jax.ShapeDtypeStruct((t, D_MODEL), F32),
                   jax.ShapeDtypeStruct((t, D_MODEL), h2_dtype)),
        grid=(t // ROW_TILE,),
        in_specs=[
            pl.BlockSpec((ROW_TILE, ATTN_WIDTH), row),
            pl.BlockSpec((ROW_TILE, KVX_WIDTH), ctx_row),
            pl.BlockSpec((ROW_TILE, KVX_WIDTH), ctx_row),
            pl.BlockSpec((dec_seq, KVX_WIDTH), lat_seq),
            pl.BlockSpec((dec_seq, KVX_WIDTH), lat_seq),
            pl.BlockSpec((None, None, past, KV_WIDTH), cache),
            pl.BlockSpec((None, None, past, KV_WIDTH), cache),
            pl.BlockSpec((ROW_TILE, POOL_WIDTH), ctx_row),
            pl.BlockSpec((dec_seq, POOL_WIDTH), lat_seq),
            *x_specs,
            pl.BlockSpec((None, None, N_MOD, D_MODEL),
                         lambda i: (l, _cond_of_tile(i, ROW_TILE, t_ctx, dec_seq), 0, 0)),
            pl.BlockSpec((None, 1, D_MODEL), lambda i: (l, 0, 0)),
            pl.BlockSpec((None, D_MODEL, D_MODEL), lambda i: (l, 0, 0)),
            pl.BlockSpec((None, N_POOL_GROUPS, POOL_GC, POOL_GC), lambda i: (l, 0, 0, 0)),
            pl.BlockSpec((None, 1, POOL_WIDTH), lambda i: (l, 0, 0)),
        ],
        out_specs=(pl.BlockSpec((ROW_TILE, D_MODEL), row),
                   pl.BlockSpec((ROW_TILE, D_MODEL), row)),
        scratch_shapes=[pltpu.VMEM((ROW_TILE, D_MODEL), BF16)],
        compiler_params=pltpu.CompilerParams(vmem_limit_bytes=VMEM_LIMIT),
        name=f"mixer_{l}",
    )(q, kx, vx, kx, vx, cache_k, cache_v, u, u, *x_ops, mod, norm_w, w_o, pool_w, pool_scale)


def _swiglu(h, w1_ref, w3_ref, w2_ref):
    total = None
    for s in range(N_SLABS):
        cols = slice(s * FF_SLAB, (s + 1) * FF_SLAB)
        a = _dot(h, w1_ref[:, cols])
        b = _dot(h, w3_ref[:, cols])
        gated = (a * jax.nn.sigmoid(a) * b).astype(BF16)
        part = _dot(gated, w2_ref[cols, :])
        total = part if total is None else total + part
    return total


def _ffn_kernel(h_ref, w1_ref, w3_ref, w2_ref, x1_ref, mod_ref, o_ref):
    o_ref[...] = x1_ref[...] + mod_ref[5:6, :] * _swiglu(h_ref[...], w1_ref, w3_ref, w2_ref)


def _dense_ffn(l, h2, x1, mod, w1, w3, w2, t_ctx, dec_seq):
    t = x1.shape[0]
    row = lambda i: (i, 0)
    once = pl.Buffered(1)
    return pl.pallas_call(
        _ffn_kernel,
        out_shape=jax.ShapeDtypeStruct((t, D_MODEL), F32),
        grid=(t // FFN_TILE,),
        in_specs=[
            pl.BlockSpec((FFN_TILE, D_MODEL), row),
            pl.BlockSpec((None, D_MODEL, D_FF), lambda i: (l // 2, 0, 0), pipeline_mode=once),
            pl.BlockSpec((None, D_MODEL, D_FF), lambda i: (l // 2, 0, 0), pipeline_mode=once),
            pl.BlockSpec((None, D_FF, D_MODEL), lambda i: (l // 2, 0, 0), pipeline_mode=once),
            pl.BlockSpec((FFN_TILE, D_MODEL), row),
            pl.BlockSpec((None, None, N_MOD, D_MODEL),
                         lambda i: (l, _cond_of_tile(i, FFN_TILE, t_ctx, dec_seq), 0, 0)),
        ],
        out_specs=pl.BlockSpec((FFN_TILE, D_MODEL), row),
        compiler_params=pltpu.CompilerParams(vmem_limit_bytes=FFN_VMEM_LIMIT),
        name=f"dense_ffn_{l}",
    )(h2, w1, w3, w2, x1, mod)


def _route_kernel(h_ref, rw_ref, rb_ref, meta_ref, wts_ref, cnt_ref, carry_ref):
    @pl.when(pl.program_id(0) == 0)
    def _():
        carry_ref[...] = jnp.zeros_like(carry_ref)

    h = h_ref[...]
    hi, lo = _split_bf16(h)
    rw = rw_ref[...]
    whi, wlo = _split_bf16(rw)
    logits = _dot(hi, whi) + _dot(lo, whi) + _dot(hi, wlo) + rb_ref[...]
    lane = lax.broadcasted_iota(jnp.int32, logits.shape, 1)
    lane_f = lane.astype(F32)
    m0 = jnp.max(logits, axis=-1, keepdims=True)
    e0 = jnp.min(jnp.where(logits == m0, lane_f, float(LANES)), axis=-1, keepdims=True)
    rest = jnp.where(lane_f == e0, -jnp.inf, logits)
    m1 = jnp.max(rest, axis=-1, keepdims=True)
    e1 = jnp.min(jnp.where(rest == m1, lane_f, float(LANES)), axis=-1, keepdims=True)
    ex = jnp.exp(m1 - m0)
    w0 = 1.0 / (1.0 + ex)
    w1 = ex / (1.0 + ex)

    n = logits.shape[0]
    sel0 = lane_f == e0
    sel1 = lane_f == e1
    mask = jnp.where(sel0 | sel1, 1.0, 0.0)
    r = lax.broadcasted_iota(jnp.int32, (n, n), 0)
    c = lax.broadcasted_iota(jnp.int32, (n, n), 1)
    below = jnp.where(c < r, 1.0, 0.0).astype(BF16)
    rank = _dot(below, mask.astype(BF16)) + carry_ref[0:1, :]
    r0 = jnp.sum(jnp.where(sel0, rank, 0.0), axis=-1, keepdims=True)
    r1 = jnp.sum(jnp.where(sel1, rank, 0.0), axis=-1, keepdims=True)
    carry_ref[...] = carry_ref[...] + jnp.sum(mask, axis=0, keepdims=True)
    cnt_ref[...] = carry_ref[...]

    cols = jnp.where(lane == 0, e0, jnp.where(lane == 1, e1, jnp.where(lane == 2, r0, jnp.where(lane == 3, r1, 0.0))))
    meta_ref[...] = cols.T[:META_ROWS, :].astype(jnp.int32)
    wts_ref[...] = jnp.where(lane == 0, w0, jnp.where(lane == 1, w1, 0.0))


def _route(h2, router_w, router_b):
    t = h2.shape[0]
    rw = jnp.zeros((D_MODEL, LANES), F32).at[:, :N_EXPERTS].set(router_w)
    rb = jnp.full((1, LANES), NEG_BIG, F32).at[0, :N_EXPERTS].set(router_b)
    row = lambda i: (i, 0)
    return pl.pallas_call(
        _route_kernel,
        out_shape=(jax.ShapeDtypeStruct((META_ROWS, t), jnp.int32),
                   jax.ShapeDtypeStruct((t, LANES), F32),
                   jax.ShapeDtypeStruct((8, LANES), F32)),
        grid=(t // ROUTE_TILE,),
        in_specs=[
            pl.BlockSpec((ROUTE_TILE, D_MODEL), row),
            pl.BlockSpec((D_MODEL, LANES), lambda i: (0, 0)),
            pl.BlockSpec((1, LANES), lambda i: (0, 0)),
        ],
        out_specs=(pl.BlockSpec((META_ROWS, ROUTE_TILE), lambda i: (0, i)),
                   pl.BlockSpec((ROUTE_TILE, LANES), row),
                   pl.BlockSpec((8, LANES), lambda i: (0, 0))),
        scratch_shapes=[pltpu.VMEM((8, LANES), F32)],
        compiler_params=pltpu.CompilerParams(dimension_semantics=("arbitrary",),
                                             vmem_limit_bytes=VMEM_LIMIT),
        name="route",
    )(h2, rw, rb)


def _row_copy(src_ref, src_row, dst_ref, dst_row, sem):
    return pltpu.make_async_copy(src_ref.at[pl.ds(src_row, 1)], dst_ref.at[pl.ds(dst_row, 1)], sem)


def _dispatch_kernel(p0_ref, p1_ref, pad_ref, h_ref, xs_ref, zero_ref, sem):
    i = pl.program_id(0)
    base = i * MOVE_TILE

    @pl.when(i == 0)
    def _():
        zero_ref[...] = jnp.zeros_like(zero_ref)
        fills = [(pad_ref[N_EXPERTS + e] == 1,
                  pltpu.make_async_copy(zero_ref, xs_ref.at[pl.ds(pl.multiple_of(pad_ref[e], FFN_TILE), FFN_TILE)], sem))
                 for e in range(N_EXPERTS)]
        n_tiles = xs_ref.shape[0] // FFN_TILE
        fills += [(tile >= pad_ref[2 * N_EXPERTS],
                   pltpu.make_async_copy(zero_ref, xs_ref.at[pl.ds(tile * FFN_TILE, FFN_TILE)], sem))
                  for tile in range(n_tiles - N_EXPERTS, n_tiles)]
        for step in ("start", "wait"):
            for wanted, fill in fills:
                @pl.when(wanted)
                def _(fill=fill, step=step):
                    getattr(fill, step)()

    def copies(b):
        group = h_ref.at[pl.ds(pl.multiple_of(b * ROW_UNROLL, ROW_UNROLL), ROW_UNROLL)]
        out = []
        for u in range(ROW_UNROLL):
            r = b * ROW_UNROLL + u
            out.append(_row_copy(group, u, xs_ref, p0_ref[base + r], sem))
            out.append(_row_copy(group, u, xs_ref, p1_ref[base + r], sem))
        return out

    def start_rows(b, carry):
        for n, cp in enumerate(copies(b)):
            cp.start(priority=n % 2)
        return carry

    def wait_rows(b, carry):
        for cp in copies(b):
            cp.wait()
        return carry

    lax.fori_loop(0, MOVE_TILE // ROW_UNROLL, start_rows, 0)
    lax.fori_loop(0, MOVE_TILE // ROW_UNROLL, wait_rows, 0)


def _dispatch(h2, pos0, pos1, pad, n_slots):
    t = h2.shape[0]
    return pl.pallas_call(
        _dispatch_kernel,
        out_shape=jax.ShapeDtypeStruct((n_slots, D_MODEL), F32),
        grid_spec=pltpu.PrefetchScalarGridSpec(
            num_scalar_prefetch=3,
            grid=(t // MOVE_TILE,),
            in_specs=[pl.BlockSpec((MOVE_TILE, D_MODEL), lambda i, p0, p1, pad: (i, 0))],
            out_specs=pl.BlockSpec(memory_space=pl.ANY),
            scratch_shapes=[pltpu.VMEM((FFN_TILE, D_MODEL), F32), pltpu.SemaphoreType.DMA],
        ),
        compiler_params=pltpu.CompilerParams(dimension_semantics=("arbitrary",),
                                             vmem_limit_bytes=VMEM_LIMIT),
        name="dispatch",
    )(pos0, pos1, pad, h2)


def _expert_kernel(te_ref, tv_ref, x_ref, w1_ref, w3_ref, w2_ref, o_ref):
    i = pl.program_id(0)

    @pl.when(tv_ref[i] == 1)
    def _():
        o_ref[...] = _swiglu(x_ref[...].astype(BF16), w1_ref, w3_ref, w2_ref)

    @pl.when(tv_ref[i] == 0)
    def _():
        o_ref[...] = jnp.zeros_like(o_ref)


def _expert_ffn(xs, tile_expert, tile_valid, w1, w3, w2):
    n_slots = xs.shape[0]
    row = lambda i, te, tv: (i, 0)
    return pl.pallas_call(
        _expert_kernel,
        out_shape=jax.ShapeDtypeStruct((n_slots, D_MODEL), F32),
        grid_spec=pltpu.PrefetchScalarGridSpec(
            num_scalar_prefetch=2,
            grid=(n_slots // FFN_TILE,),
            in_specs=[
                pl.BlockSpec((FFN_TILE, D_MODEL), row),
                pl.BlockSpec((None, D_MODEL, D_FF), lambda i, te, tv: (te[i], 0, 0)),
                pl.BlockSpec((None, D_MODEL, D_FF), lambda i, te, tv: (te[i], 0, 0)),
                pl.BlockSpec((None, D_FF, D_MODEL), lambda i, te, tv: (te[i], 0, 0)),
            ],
            out_specs=pl.BlockSpec((FFN_TILE, D_MODEL), row),
        ),
        compiler_params=pltpu.CompilerParams(vmem_limit_bytes=FFN_VMEM_LIMIT),
        name="expert_ffn",
    )(tile_expert, tile_valid, xs, w1, w3, w2)


def _combine_kernel(p0_ref, p1_ref, ys_ref, x1_ref, mod_ref, wts_ref, oc_ref, ol_ref, y0_ref, y1_ref, sem,
                    *, ctx_tiles):
    i = pl.program_id(0)
    slot = i % 2

    def rows(tile, slot, step):
        base = tile * MOVE_TILE

        def body(b, carry):
            rows8 = pl.ds(pl.multiple_of(b * ROW_UNROLL, ROW_UNROLL), ROW_UNROLL)
            group0 = y0_ref.at[slot, rows8]
            group1 = y1_ref.at[slot, rows8]
            for u in range(ROW_UNROLL):
                r = b * ROW_UNROLL + u
                first = _row_copy(ys_ref, p0_ref[base + r], group0, u, sem.at[slot])
                second = _row_copy(ys_ref, p1_ref[base + r], group1, u, sem.at[slot])
                if step == "start":
                    first.start(priority=0)
                    second.start(priority=1)
                else:
                    first.wait()
                    second.wait()
            return carry

        lax.fori_loop(0, MOVE_TILE // ROW_UNROLL, body, 0)

    @pl.when(i == 0)
    def _():
        rows(0, 0, "start")

    @pl.when(i + 1 < pl.num_programs(0))
    def _():
        rows(i + 1, 1 - slot, "start")

    rows(i, slot, "wait")
    wts = wts_ref[...]
    moe = wts[:, 0:1] * y0_ref[slot] + wts[:, 1:2] * y1_ref[slot]
    out = x1_ref[...] + mod_ref[5:6, :] * moe

    @pl.when(i < ctx_tiles)
    def _():
        oc_ref[...] = out

    @pl.when(i >= ctx_tiles)
    def _():
        ol_ref[...] = out


def _combine(l, ys, pos0, pos1, x1, mod, wts, t_ctx, dec_seq):
    t = x1.shape[0]
    ctx_tiles = t_ctx // MOVE_TILE
    return pl.pallas_call(
        functools.partial(_combine_kernel, ctx_tiles=ctx_tiles),
        out_shape=(jax.ShapeDtypeStruct((t_ctx, D_MODEL), F32),
                   jax.ShapeDtypeStruct((t - t_ctx, D_MODEL), F32)),
        grid_spec=pltpu.PrefetchScalarGridSpec(
            num_scalar_prefetch=2,
            grid=(t // MOVE_TILE,),
            in_specs=[
                pl.BlockSpec(memory_space=pl.ANY),
                pl.BlockSpec((MOVE_TILE, D_MODEL), lambda i, p0, p1: (i, 0)),
                pl.BlockSpec((None, None, N_MOD, D_MODEL),
                             lambda i, p0, p1: (l, _cond_of_tile(i, MOVE_TILE, t_ctx, dec_seq), 0, 0)),
                pl.BlockSpec((MOVE_TILE, LANES), lambda i, p0, p1: (i, 0)),
            ],
            out_specs=(pl.BlockSpec((MOVE_TILE, D_MODEL), lambda i, p0, p1: (jnp.minimum(i, ctx_tiles - 1), 0)),
                       pl.BlockSpec((MOVE_TILE, D_MODEL), lambda i, p0, p1: (jnp.maximum(i - ctx_tiles, 0), 0))),
            scratch_shapes=[pltpu.VMEM((2, MOVE_TILE, D_MODEL), F32),
                            pltpu.VMEM((2, MOVE_TILE, D_MODEL), F32),
                            pltpu.SemaphoreType.DMA((2,))],
        ),
        compiler_params=pltpu.CompilerParams(dimension_semantics=("arbitrary",),
                                             vmem_limit_bytes=VMEM_LIMIT),
        name="combine",
    )(pos0, pos1, ys, x1, mod, wts)


def _moe(l, h2, x1, mod, router_w, router_b, w1, w3, w2, t_ctx, dec_seq):
    t = h2.shape[0]
    meta, wts, counts = _route(h2, router_w, router_b)
    counts = counts[0, :N_EXPERTS].astype(jnp.int32)
    tiles_per_expert = (counts + FFN_TILE - 1) // FFN_TILE
    tile_end = jnp.cumsum(tiles_per_expert)
    slot_start = (tile_end - tiles_per_expert) * FFN_TILE
    n_tiles = (2 * t) // FFN_TILE + N_EXPERTS
    tile_id = jnp.arange(n_tiles, dtype=jnp.int32)
    tile_expert = jnp.minimum(jnp.sum(tile_id[:, None] >= tile_end[None, :], axis=1), N_EXPERTS - 1)
    tile_valid = (tile_id < tile_end[-1]).astype(jnp.int32)

    def slot_of(expert, rank):
        start = sum(jnp.where(expert == e, slot_start[e], 0) for e in range(N_EXPERTS))
        return start + rank

    pos0 = slot_of(meta[0], meta[2])
    pos1 = slot_of(meta[1], meta[3])
    pad = jnp.concatenate([jnp.maximum(tile_end - 1, 0) * FFN_TILE, (tiles_per_expert > 0).astype(jnp.int32),
                           tile_end[-1:]])
    xs = _dispatch(h2, pos0, pos1, pad.astype(jnp.int32), n_tiles * FFN_TILE)
    ys = _expert_ffn(xs, tile_expert.astype(jnp.int32), tile_valid, w1, w3, w2)
    return _combine(l, ys, pos0, pos1, x1, mod, wts, t_ctx, dec_seq)


def _rope_tables(dec_seq):
    pos = np.arange(dec_seq)
    row = (pos // GRID_W).astype(np.float32)
    col = (pos % GRID_W).astype(np.float32)
    nf = HEAD_DIM // 4
    inv = np.float32(ROPE_THETA) ** (-np.arange(nf, dtype=np.float32) / np.float32(nf))
    ang_r = row[:, None] * inv
    ang_c = col[:, None] * inv
    cos = np.concatenate([np.cos(ang_r)] * 2 + [np.cos(ang_c)] * 2, axis=1)
    sin = np.concatenate([-np.sin(ang_r), np.sin(ang_r), -np.sin(ang_c), np.sin(ang_c)], axis=1)
    cos = np.concatenate([np.ones((ROW_TILE, HEAD_DIM), np.float32), cos], axis=0)
    sin = np.concatenate([np.zeros((ROW_TILE, HEAD_DIM), np.float32), sin], axis=0)
    return (jnp.asarray(np.tile(cos, (1, N_HEADS)).astype(np.float32)),
            jnp.asarray(np.tile(sin, (1, N_HEADS)).astype(np.float32)))


def kernel(x_prompt, x_sample, cache_k, cache_v, c, c_ctx, norm1_w, norm2_w, ada_w, ada_b, w_in, w_o,
           q_norm_w, k_norm_w, pool_w, pool_scale, ffn_w1, ffn_w3, ffn_w2, router_w, router_b,
           moe_w1, moe_w3, moe_w2):
    n_ctx, seq, _ = x_prompt.shape
    n_lat, dec_seq, _ = x_sample.shape
    depth = w_in.shape[0]
    past = cache_k.shape[2]
    t_ctx = n_ctx * seq
    assert seq == SEQ_TILE and dec_seq % ROW_TILE == 0 and t_ctx % dec_seq == 0
    assert t_ctx % FFN_TILE == 0 and dec_seq % FFN_TILE == 0 and 1 + n_lat <= COND_ROWS

    t = t_ctx + n_lat * dec_seq
    x = (x_prompt.reshape(t_ctx, D_MODEL), x_sample.reshape(n_lat * dec_seq, D_MODEL))
    cond = jnp.zeros((COND_ROWS, D_MODEL), F32).at[0].set(c_ctx).at[1:1 + n_lat].set(c)
    mod = _ada(cond, ada_w, ada_b).reshape(depth, COND_ROWS, N_MOD, D_MODEL)

    cos_t, sin_t = _rope_tables(dec_seq)
    head = np.arange(ATTN_WIDTH) // HEAD_DIM
    bd = jnp.asarray(head[:, None] == head[None, :], dtype=BF16)
    qw = jnp.tile(q_norm_w, (1, N_HEADS)).reshape(depth, 1, ATTN_WIDTH)
    kw = jnp.tile(k_norm_w, (1, N_KV_HEADS)).reshape(depth, 1, KV_WIDTH)
    n1 = norm1_w.reshape(depth, 1, D_MODEL)
    n2 = norm2_w.reshape(depth, 1, D_MODEL)
    ck = cache_k.reshape(n_lat, depth, past, KV_WIDTH)
    cv = cache_v.reshape(n_lat, depth, past, KV_WIDTH)
    w_in = w_in.astype(BF16)
    w_o = w_o.astype(BF16)
    pool_w = pool_w.astype(BF16)
    pool_s = pool_scale.reshape(depth, 1, POOL_WIDTH)

    new_k, new_v = [], []
    for l in range(depth):
        q, kx, vx, k, v, u = _in_proj(l, x, t, mod, n1, w_in, qw, kw, bd, cos_t, sin_t, t_ctx, dec_seq)
        new_k.append(k.reshape(n_ctx, seq, N_KV_HEADS, HEAD_DIM))
        new_v.append(v.reshape(n_ctx, seq, N_KV_HEADS, HEAD_DIM))
        h2_dtype = BF16 if l % 2 == 0 else F32
        x1, h2 = _mixer(l, q, kx, vx, ck, cv, u, x, mod, n2, w_o, pool_w, pool_s, h2_dtype, t_ctx, dec_seq)
        if l % 2 == 0:
            x = _dense_ffn(l, h2, x1, mod, ffn_w1.astype(BF16), ffn_w3.astype(BF16), ffn_w2.astype(BF16),
                           t_ctx, dec_seq)
        else:
            x = _moe(l, h2, x1, mod, router_w[l // 2], router_b[l // 2],
                     moe_w1[l // 2].astype(BF16), moe_w3[l // 2].astype(BF16), moe_w2[l // 2].astype(BF16),
                     t_ctx, dec_seq)

    y_ctx, y_lat = x if isinstance(x, tuple) else (x[:t_ctx], x[t_ctx:])
    y_prompt = y_ctx.reshape(n_ctx, seq, D_MODEL)
    y_sample = y_lat.reshape(n_lat, dec_seq, D_MODEL)
    return y_prompt, y_sample, jnp.stack(new_k, axis=1), jnp.stack(new_v, axis=1)
```
